```python
import math
import jax, jax.numpy as jnp
from jax import lax
import numpy as np

D_MODEL = 1024
BATCH = 8
SEQ = 4096
DEPTH = 4

GRID_W = 64
CTX_LEN = 256
N_MOD = 6
NORM_EPS = 1e-6

MLA_HEADS = 8
MLA_Q_RANK = 384
MLA_KV_RANK = 256
MLA_NOPE = 64
MLA_ROPE = 32
MLA_V = 64
ROPE_THETA = 10000.0
Q_BLOCK = 128

GDN_HEADS = 4
GDN_DK = 64
GDN_DV = 64
GDN_CONV = 3
GDN_CHUNK = 64

GLA_HEADS = 4
GLA_DK = 32
GLA_DV = 64
GLA_GATE_RANK = 16
GLA_NORMALIZER = 16.0
GLA_CHUNK = 16

FFN_HIDDEN = 2560
FFN_CONV = 3

D_MIX = MLA_HEADS * MLA_V + GDN_HEADS * GDN_DV + GLA_HEADS * GLA_DV
GDN_QKV = GDN_HEADS * (2 * GDN_DK + GDN_DV)
IN_SIZES = (MLA_Q_RANK, MLA_KV_RANK, MLA_ROPE,
            GDN_QKV, GDN_HEADS * GDN_DV, 2 * GDN_HEADS, 2 * GDN_HEADS,
            GLA_HEADS * GLA_DK, GLA_HEADS * GLA_DK, GLA_HEADS * GLA_DV, GLA_HEADS * GLA_DV,
            2 * GLA_GATE_RANK)
IN_SPLITS = tuple(int(s) for s in np.cumsum(IN_SIZES)[:-1])
D_IN_PROJ = sum(IN_SIZES)

kernel_name = 'hybrid_mla_gdn_gla_dit_block'

F32 = jnp.float32


def rmsnorm(x, g):
    xf = x.astype(F32)
    y = xf * lax.rsqrt(jnp.mean(xf * xf, axis=-1, keepdims=True) + NORM_EPS)
    return y.astype(x.dtype) * g


def l2norm(x):
    xf = x.astype(F32)
    return (xf * lax.rsqrt(jnp.sum(xf * xf, axis=-1, keepdims=True) + NORM_EPS)).astype(x.dtype)


def modulate(h, shift, scale):
    return h * (1 + scale) + shift


def dwconv_centred(x, w):
    K = w.shape[0]
    return lax.conv_general_dilated(
        x, w[:, None, :].astype(x.dtype), window_strides=(1,), padding=[(K // 2, K // 2)],
        dimension_numbers=('NWC', 'WIO', 'NWC'), feature_group_count=x.shape[-1])


def axial_angles(T):
    rows = T // GRID_W
    t = jnp.arange(rows * GRID_W)
    row = (t // GRID_W).astype(F32)
    col = (t % GRID_W).astype(F32)
    n = MLA_ROPE // 4
    inv = ROPE_THETA ** (-jnp.arange(n, dtype=F32) / n)
    ang = jnp.stack([row[:, None] * inv, col[:, None] * inv], axis=1)
    return jnp.cos(ang), jnp.sin(ang)


def apply_axial_rope(x, cos, sin):
    xr = x.reshape(*x.shape[:-1], 2, 2, MLA_ROPE // 4)
    x1, x2 = xr[..., 0, :], xr[..., 1, :]
    cos, sin = cos.astype(x.dtype), sin.astype(x.dtype)
    out = jnp.stack([x1 * cos - x2 * sin, x2 * cos + x1 * sin], axis=-2)
    return out.reshape(x.shape)


def mla_project(parts, q_norm, w_uq, kv_norm, w_ukv):
    cq, ckv, k_rope = parts[0], parts[1], parts[2]
    B, L = cq.shape[:2]
    q = (rmsnorm(cq, q_norm) @ w_uq).reshape(B, L, MLA_HEADS, MLA_NOPE + MLA_ROPE)
    kv = (rmsnorm(ckv, kv_norm) @ w_ukv).reshape(B, L, MLA_HEADS, MLA_NOPE + MLA_V)
    return q[..., :MLA_NOPE], q[..., MLA_NOPE:], kv[..., :MLA_NOPE], k_rope, kv[..., MLA_NOPE:]


def mla_attention(q_nope, q_rope, k_nope, k_rope, v):
    scale = (MLA_NOPE + MLA_ROPE) ** -0.5
    s = (jnp.einsum('bqhd,bkhd->bhqk', q_nope, k_nope)
         + jnp.einsum('bqhd,bkd->bhqk', q_rope, k_rope))
    p = jax.nn.softmax(s.astype(F32) * scale, axis=-1).astype(v.dtype)
    return jnp.einsum('bhqk,bkhd->bqhd', p, v)


def mla_attention_blocked(q_nope, q_rope, k_nope, k_rope, v):
    B, T = q_nope.shape[:2]
    nb = T // Q_BLOCK

    def blocks(a):
        return jnp.moveaxis(a.reshape(B, nb, Q_BLOCK, *a.shape[2:]), 1, 0)

    out = lax.map(lambda qs: mla_attention(qs[0], qs[1], k_nope, k_rope, v),
                  (blocks(q_nope), blocks(q_rope)))
    return jnp.moveaxis(out, 0, 1).reshape(B, T, *out.shape[3:])


def gated_delta_chunked(q, k, v, g, beta, S0):
    dt = v.dtype
    B, H, T, Dk = q.shape
    Dv = v.shape[-1]
    C = GDN_CHUNK
    N = T // C

    def f(a):
        return a.astype(F32).reshape(B, H, N, C, *a.shape[3:])

    q, k, v, g, beta = f(q) * Dk ** -0.5, f(k), f(v), f(g), f(beta)
    gc = jnp.cumsum(g, axis=-1)
    incl = jnp.tril(jnp.ones((C, C), bool))
    strict = jnp.tril(jnp.ones((C, C), bool), -1)
    diff = gc[..., :, None] - gc[..., None, :]
    decay = jnp.where(incl, jnp.exp(jnp.minimum(diff, 0.0)), 0.0)
    kb = k * beta[..., None]
    L = jnp.where(strict, jnp.einsum('bhnik,bhnjk->bhnij', kb, k) * decay, 0.0)
    Tinv = jnp.eye(C, dtype=F32) - L
    P = L
    for _ in range(int(math.log2(C)) - 1):
        P = P @ P
        Tinv = Tinv + Tinv @ P
    u = Tinv @ (v * beta[..., None])
    w = Tinv @ (kb * jnp.exp(gc)[..., None])
    A = jnp.einsum('bhnik,bhnjk->bhnij', q, k) * decay
    qd = q * jnp.exp(gc)[..., None]
    kend = k * jnp.exp(gc[..., -1:] - gc)[..., None]
    glast = jnp.exp(gc[..., -1])

    def step(S, xs):
        qd_, w_, u_, A_, kend_, glast_ = xs
        v_new = u_ - jnp.einsum('bhck,bhkv->bhcv', w_, S)
        o = jnp.einsum('bhck,bhkv->bhcv', qd_, S) + jnp.einsum('bhij,bhjv->bhiv', A_, v_new)
        S = S * glast_[..., None, None] + jnp.einsum('bhck,bhcv->bhkv', kend_, v_new)
        return S, o

    xs = tuple(jnp.moveaxis(a, 2, 0) for a in (qd, w, u, A, kend, glast))
    S, o = lax.scan(step, S0.astype(F32), xs)
    o = jnp.moveaxis(o, 0, 2).reshape(B, H, T, Dv)
    return o.astype(dt), S


def gla_chunked(q, k, v, la, S0):
    dt = v.dtype
    B, H, T, Dk = q.shape
    Dv = v.shape[-1]
    C = GLA_CHUNK
    N = T // C

    def f(a):
        return a.astype(F32).reshape(B, H, N, C, *a.shape[3:])

    q, k, v, la = f(q) * Dk ** -0.5, f(k), f(v), f(la)
    b = jnp.cumsum(la, axis=3)
    incl = jnp.tril(jnp.ones((C, C), bool))[:, :, None]
    diff = b[:, :, :, :, None, :] - b[:, :, :, None, :, :]
    decay = jnp.where(incl, jnp.exp(jnp.minimum(diff, 0.0)), 0.0)
    A = jnp.einsum('bhnik,bhnjk,bhnijk->bhnij', q, k, decay)
    o_intra = jnp.einsum('bhnij,bhnjv->bhniv', A, v)
    qd = q * jnp.exp(b)
    kend = k * jnp.exp(b[..., -1:, :] - b)
    glast = jnp.exp(b[..., -1, :])

    def step(S, xs):
        qd_, kend_, v_, glast_ = xs
        o = jnp.einsum('bhck,bhkv->bhcv', qd_, S)
        S = S * glast_[..., :, None] + jnp.einsum('bhck,bhcv->bhkv', kend_, v_)
        return S, o

    xs = tuple(jnp.moveaxis(a, 2, 0) for a in (qd, kend, v, glast))
    S, o_inter = lax.scan(step, S0.astype(F32), xs)
    o = o_intra + jnp.moveaxis(o_inter, 0, 2)
    return o.reshape(B, H, T, Dv).astype(dt), S


def rev(a, d):
    return jnp.flip(a, axis=2) if d else a


def bidirectional_scan(chunk_fn, seq_c, dir_c, seq_x, dir_x, state_shape):
    outs_c, outs_x = [], []
    for d in range(2):
        S0 = jnp.zeros(state_shape, F32)
        o_c, S_c = chunk_fn(*[rev(a, d) for a in seq_c], *[rev(a[d], d) for a in dir_c], S0)
        o_x, _ = chunk_fn(*[rev(a, d) for a in seq_x], *[rev(a[d], d) for a in dir_x], S_c)
        outs_c.append(rev(o_c, d))
        outs_x.append(rev(o_x, d))
    return outs_c[0] + outs_c[1], outs_x[0] + outs_x[1]


def gdn_features(parts, conv_w, a_log, dt_bias):
    qkv = jax.nn.silu(dwconv_centred(parts[3], conv_w))
    B, L = qkv.shape[:2]
    q, k, v = jnp.split(qkv, [GDN_HEADS * GDN_DK, 2 * GDN_HEADS * GDN_DK], axis=-1)

    def heads(a, d):
        return jnp.moveaxis(a.reshape(B, L, GDN_HEADS, d), 2, 1)

    q, k, v = l2norm(heads(q, GDN_DK)), l2norm(heads(k, GDN_DK)), heads(v, GDN_DV)
    a = parts[5].reshape(B, L, 2, GDN_HEADS).astype(F32)
    bb = parts[6].reshape(B, L, 2, GDN_HEADS).astype(F32)
    g = -jnp.exp(a_log.astype(F32)) * jax.nn.softplus(a + dt_bias.astype(F32))
    beta = jax.nn.sigmoid(bb)
    g = jnp.transpose(g, (2, 0, 3, 1))
    beta = jnp.transpose(beta, (2, 0, 3, 1))
    return (q, k, v), (g, beta), parts[4]


def gla_features(parts, w_gk, b_gk):
    B, L = parts[7].shape[:2]

    def heads(a, d):
        return jnp.moveaxis(a.reshape(B, L, GLA_HEADS, d), 2, 1)

    q, k, v = heads(parts[7], GLA_DK), heads(parts[8], GLA_DK), heads(parts[9], GLA_DV)
    lr = parts[11].reshape(B, L, 2, GLA_GATE_RANK)
    gk = jnp.einsum('bldr,drk->dblk', lr, w_gk) + b_gk[:, None, None, :]
    la = jax.nn.log_sigmoid(gk.astype(F32)) / GLA_NORMALIZER
    la = jnp.moveaxis(la.reshape(2, B, L, GLA_HEADS, GLA_DK), 3, 2)
    return (q, k, v), (la,), parts[10]


def merge_heads(mla_o, gdn_o, gdn_z, gdn_norm, gla_o, gla_g, gla_norm):
    B, L = mla_o.shape[:2]
    gdn_o = rmsnorm(jnp.moveaxis(gdn_o, 1, 2), gdn_norm) * jax.nn.silu(gdn_z.reshape(B, L, GDN_HEADS, GDN_DV))
    gla_o = rmsnorm(jnp.moveaxis(gla_o, 1, 2), gla_norm) * jax.nn.silu(gla_g.reshape(B, L, GLA_HEADS, GLA_DV))
    return jnp.concatenate([mla_o.reshape(B, L, -1), gdn_o.reshape(B, L, -1), gla_o.reshape(B, L, -1)], axis=-1)


def token_mixing(hc, hx, w_in, mla_q_norm, mla_w_uq, mla_kv_norm, mla_w_ukv,
                 gdn_conv_w, gdn_a_log, gdn_dt_bias, gdn_norm,
                 gla_w_gk, gla_b_gk, gla_norm, w_out, need_ctx_out):
    B, T, _ = hx.shape
    pc = jnp.split(hc @ w_in, IN_SPLITS, axis=-1)
    px = jnp.split(hx @ w_in, IN_SPLITS, axis=-1)

    qn_c, qr_c, kn_c, kr_c, v_c = mla_project(pc, mla_q_norm, mla_w_uq, mla_kv_norm, mla_w_ukv)
    qn_x, qr_x, kn_x, kr_x, v_x = mla_project(px, mla_q_norm, mla_w_uq, mla_kv_norm, mla_w_ukv)
    cos, sin = axial_angles(T)
    qr_x = apply_axial_rope(qr_x, cos[:, None], sin[:, None])
    kr_x = apply_axial_rope(kr_x, cos, sin)

    def cat(a, b):
        return jnp.concatenate([a, b], axis=1)

    mla_x = mla_attention_blocked(qn_x, qr_x, cat(kn_c, kn_x), cat(kr_c, kr_x), cat(v_c, v_x))

    seq_c, dir_c, z_c = gdn_features(pc, gdn_conv_w, gdn_a_log, gdn_dt_bias)
    seq_x, dir_x, z_x = gdn_features(px, gdn_conv_w, gdn_a_log, gdn_dt_bias)
    gdn_c, gdn_x = bidirectional_scan(gated_delta_chunked, seq_c, dir_c, seq_x, dir_x,
                                      (B, GDN_HEADS, GDN_DK, GDN_DV))

    sq_c, dr_c, gg_c = gla_features(pc, gla_w_gk, gla_b_gk)
    sq_x, dr_x, gg_x = gla_features(px, gla_w_gk, gla_b_gk)
    gla_c, gla_x = bidirectional_scan(gla_chunked, sq_c, dr_c, sq_x, dr_x,
                                      (B, GLA_HEADS, GLA_DK, GLA_DV))

    y_x = merge_heads(mla_x, gdn_x, z_x, gdn_norm, gla_x, gg_x, gla_norm) @ w_out
    y_c = None
    if need_ctx_out:
        mla_c = mla_attention(qn_c, qr_c, kn_c, kr_c, v_c)
        y_c = merge_heads(mla_c, gdn_c, z_c, gdn_norm, gla_c, gg_c, gla_norm) @ w_out
    return y_c, y_x


def conv_ffn(h, w_in, b_in, conv_w, conv_b, w_out):
    z = dwconv_centred(h @ w_in + b_in, conv_w) + conv_b
    a, gt = jnp.split(z, 2, axis=-1)
    return (a * jax.nn.silu(gt)) @ w_out


def setup_inputs(seed: int = 0) -> dict:
    key = jax.random.key(seed)
    ks = jax.random.split(key, 28)
    L = DEPTH

    def nrm(k, shape, scale):
        return jax.random.normal(k, shape, F32) * scale

    def gain(k, shape):
        return 1.0 + 0.05 * jax.random.normal(k, shape, F32)

    dt0 = jax.random.uniform(ks[17], (L, 2, GDN_HEADS), F32, 0.001, 0.1)
    return {
        'x': nrm(ks[0], (BATCH, SEQ, D_MODEL), 1.0),
        'c': nrm(ks[1], (BATCH, D_MODEL), 1.0),
        'ctx': nrm(ks[2], (BATCH, CTX_LEN, D_MODEL), 1.0),
        'c_ctx': nrm(ks[3], (D_MODEL,), 1.0),
        'w_ada': nrm(ks[4], (L, D_MODEL, N_MOD * D_MODEL), 0.5 * D_MODEL ** -0.5),
        'b_ada': nrm(ks[5], (L, N_MOD * D_MODEL), 0.02),
        'norm_mix_pre': gain(ks[6], (L, D_MODEL)),
        'norm_mix_post': gain(ks[7], (L, D_MODEL)),
        'norm_ffn_pre': gain(ks[8], (L, D_MODEL)),
        'norm_ffn_post': gain(ks[9], (L, D_MODEL)),
        'w_in': nrm(ks[10], (L, D_MODEL, D_IN_PROJ), D_MODEL ** -0.5),
        'mla_q_norm': gain(ks[11], (L, MLA_Q_RANK)),
        'mla_w_uq': nrm(ks[12], (L, MLA_Q_RANK, MLA_HEADS * (MLA_NOPE + MLA_ROPE)), MLA_Q_RANK ** -0.5),
        'mla_kv_norm': gain(ks[13], (L, MLA_KV_RANK)),
        'mla_w_ukv': nrm(ks[14], (L, MLA_KV_RANK, MLA_HEADS * (MLA_NOPE + MLA_V)), MLA_KV_RANK ** -0.5),
        'gdn_conv_w': nrm(ks[15], (L, GDN_CONV, GDN_QKV), GDN_CONV ** -0.5),
        'gdn_a_log': jnp.log(jax.random.uniform(ks[16], (L, 2, GDN_HEADS), F32, 1.0, 16.0)),
        'gdn_dt_bias': jnp.log(jnp.expm1(dt0)),
        'gdn_norm': gain(ks[18], (L, GDN_DV)),
        'gla_w_gk': nrm(ks[19], (L, 2, GLA_GATE_RANK, GLA_HEADS * GLA_DK), GLA_GATE_RANK ** -0.5),
        'gla_b_gk': nrm(ks[20], (L, 2, GLA_HEADS * GLA_DK), 0.1),
        'gla_norm': gain(ks[21], (L, GLA_DV)),
        'w_out': nrm(ks[22], (L, D_MIX, D_MODEL), D_MIX ** -0.5),
        'ffn_w_in': nrm(ks[23], (L, D_MODEL, 2 * FFN_HIDDEN), D_MODEL ** -0.5),
        'ffn_b_in': nrm(ks[24], (L, 2 * FFN_HIDDEN), 0.02),
        'ffn_conv_w': nrm(ks[25], (L, FFN_CONV, 2 * FFN_HIDDEN), FFN_CONV ** -0.5),
        'ffn_conv_b': nrm(ks[26], (L, 2 * FFN_HIDDEN), 0.02),
        'ffn_w_out': nrm(ks[27], (L, FFN_HIDDEN, D_MODEL), FFN_HIDDEN ** -0.5),
    }


def reference(x, c, ctx, c_ctx, w_ada, b_ada, norm_mix_pre, norm_mix_post, norm_ffn_pre, norm_ffn_post,
              w_in, mla_q_norm, mla_w_uq, mla_kv_norm, mla_w_ukv, gdn_conv_w, gdn_a_log, gdn_dt_bias,
              gdn_norm, gla_w_gk, gla_b_gk, gla_norm, w_out, ffn_w_in, ffn_b_in, ffn_conv_w, ffn_conv_b,
              ffn_w_out):
    xc = ctx
    for i in range(DEPTH):
        last = i == DEPTH - 1
        mod_x = [m[:, None, :] for m in jnp.split(jax.nn.silu(c) @ w_ada[i] + b_ada[i], N_MOD, axis=-1)]
        mod_c = jnp.split(jax.nn.silu(c_ctx) @ w_ada[i] + b_ada[i], N_MOD, axis=-1)

        hx = modulate(rmsnorm(x, norm_mix_pre[i]), mod_x[0], mod_x[1])
        hc = modulate(rmsnorm(xc, norm_mix_pre[i]), mod_c[0], mod_c[1])
        y_c, y_x = token_mixing(hc, hx, w_in[i], mla_q_norm[i], mla_w_uq[i], mla_kv_norm[i], mla_w_ukv[i],
                                gdn_conv_w[i], gdn_a_log[i], gdn_dt_bias[i], gdn_norm[i],
                                gla_w_gk[i], gla_b_gk[i], gla_norm[i], w_out[i], not last)
        x = x + mod_x[2] * rmsnorm(y_x, norm_mix_post[i])

        hx = modulate(rmsnorm(x, norm_ffn_pre[i]), mod_x[3], mod_x[4])
        x = x + mod_x[5] * rmsnorm(conv_ffn(hx, ffn_w_in[i], ffn_b_in[i], ffn_conv_w[i], ffn_conv_b[i],
                                            ffn_w_out[i]), norm_ffn_post[i])

        if not last:
            xc = xc + mod_c[2] * rmsnorm(y_c, norm_mix_post[i])
            hc = modulate(rmsnorm(xc, norm_ffn_pre[i]), mod_c[3], mod_c[4])
            xc = xc + mod_c[5] * rmsnorm(conv_ffn(hc, ffn_w_in[i], ffn_b_in[i], ffn_conv_w[i], ffn_conv_b[i],
                                                  ffn_w_out[i]), norm_ffn_post[i])
    return x
```

```python
import functools
import math

import numpy as np
import jax
import jax.numpy as jnp
from jax import lax
from jax.experimental import pallas as pl
from jax.experimental.pallas import tpu as pltpu

F32 = jnp.float32
BF16 = jnp.bfloat16

D_MODEL = 1024
DEPTH = 4
GRID_W = 64
CTX_LEN = 256
N_MOD = 6
NORM_EPS = 1e-6

MLA_HEADS = 8
MLA_Q_RANK = 384
MLA_KV_RANK = 256
MLA_NOPE = 64
MLA_ROPE = 32
MLA_V = 64
ROPE_THETA = 10000.0

GDN_HEADS = 4
GDN_DK = 64
GDN_DV = 64
GDN_CHUNK = 64

GLA_HEADS = 4
GLA_DK = 32
GLA_DV = 64
GLA_GATE_RANK = 16
GLA_NORMALIZER = 16.0
GLA_CHUNK = 16

FFN_HIDDEN = 2560
D_MIX = MLA_HEADS * MLA_V + GDN_HEADS * GDN_DV + GLA_HEADS * GLA_DV
GDN_QKV = GDN_HEADS * (2 * GDN_DK + GDN_DV)
IN_SIZES = (MLA_Q_RANK, MLA_KV_RANK, MLA_ROPE,
            GDN_QKV, GDN_HEADS * GDN_DV, 2 * GDN_HEADS, 2 * GDN_HEADS,
            GLA_HEADS * GLA_DK, GLA_HEADS * GLA_DK, GLA_HEADS * GLA_DV, GLA_HEADS * GLA_DV,
            2 * GLA_GATE_RANK)
IN_OFFS = tuple(int(s) for s in np.cumsum((0,) + IN_SIZES))

LANE = 128
SUBLANE = 8
TM = CTX_LEN
HALO = SUBLANE
HEAD_PAD = LANE
FFN_CHUNK = 512
VMEM_LIMIT = 56 * 1024 * 1024

P_CQ = 0
P_CKV = P_CQ + MLA_Q_RANK
P_GQKV = P_CKV + MLA_KV_RANK
P_GZ = P_GQKV + GDN_QKV
P_LQ = P_GZ + GDN_HEADS * GDN_DV
P_LK = P_LQ + GLA_HEADS * GLA_DK
P_LV = P_LK + GLA_HEADS * GLA_DK
P_LG = P_LV + GLA_HEADS * GLA_DV
P_MISC = P_LG + GLA_HEADS * GLA_DV
P_TOTAL = P_MISC + LANE
M_KR = 0
M_A = MLA_ROPE
M_B = M_A + 2 * GDN_HEADS
M_LR = M_B + 2 * GDN_HEADS


def _cparams(sem, vmem=VMEM_LIMIT):
    return pltpu.CompilerParams(dimension_semantics=sem, vmem_limit_bytes=vmem)


def _rms(x, g):
    return x * lax.rsqrt(jnp.mean(x * x, axis=-1, keepdims=True) + NORM_EPS) * g


def _silu(x):
    return x * jax.nn.sigmoid(x)


def _dot(a, b):
    return jnp.dot(a, b, preferred_element_type=F32)


def _dot_t(a, b):
    return lax.dot_general(a, b, (((1,), (1,)), ((), ())), preferred_element_type=F32)


def _ada_kernel(c_ref, w_ref, b_ref, o_ref):
    o_ref[0] = _dot(_silu(c_ref[...]), w_ref[0]) + b_ref[0]


def ada_modulation(cvec, w_ada, b_ada):
    depth, d, n = w_ada.shape
    rows = cvec.shape[0]
    tn = n // 4
    return pl.pallas_call(
        _ada_kernel,
        out_shape=jax.ShapeDtypeStruct((depth, rows, n), F32),
        grid=(depth, n // tn),
        in_specs=[pl.BlockSpec((rows, d), lambda l, j: (0, 0)),
                  pl.BlockSpec((1, d, tn), lambda l, j: (l, 0, j)),
                  pl.BlockSpec((1, 1, tn), lambda l, j: (l, 0, j))],
        out_specs=pl.BlockSpec((1, rows, tn), lambda l, j: (l, 0, j)),
        compiler_params=_cparams(("arbitrary", "arbitrary")),
        name="ada_modulation",
    )(cvec, w_ada, b_ada.reshape(depth, 1, n))


def _pre_mix_kernel(x_ref, mod_ref, g_ref, w_in_ref, qn_ref, kvn_ref, w_uq_ref, w_kT_ref, w_v_ref,
                    w_krT_ref, cq_ref, sq_ref, ckT_ref, skT_ref,
                    q_ref, kT_ref, v_ref, gqkv_ref, gz_ref, lq_ref, lk_ref, lv_ref, lg_ref, misc_ref):
    x = x_ref[0]
    shift = mod_ref[0, 0:1, :]
    scale = mod_ref[0, 1:2, :]
    h = _rms(x, g_ref[...]) * (1.0 + scale) + shift
    hb = h.astype(BF16)
    p = _dot(hb, w_in_ref[...])

    gqkv_ref[0] = p[:, P_GQKV:P_GZ]
    gz_ref[0] = p[:, P_GZ:P_LQ]
    lq_ref[0] = p[:, P_LQ:P_LK]
    lk_ref[0] = p[:, P_LK:P_LV]
    lv_ref[0] = p[:, P_LV:P_LG]
    lg_ref[0] = p[:, P_LG:P_MISC]
    misc_ref[0] = p[:, P_MISC:P_TOTAL]

    cqn = _rms(p[:, P_CQ:P_CKV], qn_ref[...]).astype(BF16)
    ckvn = _rms(p[:, P_CKV:P_GQKV], kvn_ref[...]).astype(BF16)
    nq = MLA_HEADS * HEAD_PAD
    qf = _dot(cqn, w_uq_ref[...])
    q = qf[:, :nq] * cq_ref[...] + qf[:, nq:] * sq_ref[...]
    vf = _dot(ckvn, w_v_ref[...])
    one_lane = lax.broadcasted_iota(jnp.int32, (1, HEAD_PAD), 1) == MLA_V
    kTn = _dot_t(w_kT_ref[...], ckvn)
    krT = _dot_t(w_krT_ref[...], hb)
    kr = (krT[:MLA_ROPE] * ckT_ref[...] + krT[MLA_ROPE:] * skT_ref[...]).astype(BF16)
    zpad = jnp.zeros((HEAD_PAD - MLA_NOPE - MLA_ROPE, kr.shape[1]), BF16)
    for hd in range(MLA_HEADS):
        q_ref[0, hd] = q[:, hd * HEAD_PAD:(hd + 1) * HEAD_PAD].astype(BF16)
        vh = vf[:, hd * HEAD_PAD:(hd + 1) * HEAD_PAD]
        v_ref[0, hd] = jnp.where(one_lane, 1.0, vh).astype(BF16)
        kT_ref[0, hd, 0, 0:MLA_NOPE, :] = kTn[hd * MLA_NOPE:(hd + 1) * MLA_NOPE].astype(BF16)
        kT_ref[0, hd, 0, MLA_NOPE:MLA_NOPE + MLA_ROPE, :] = kr
        kT_ref[0, hd, 0, MLA_NOPE + MLA_ROPE:, :] = zpad


def pre_mix(x, mods, norm_g, wts, tabs):
    bsz, seq, d = x.shape
    nt = seq // TM
    const = lambda *shape: pl.BlockSpec(shape, lambda i, b: (0,) * len(shape))
    tile = lambda w: pl.BlockSpec((1, TM, w), lambda i, b: (b, i, 0))
    nq = MLA_HEADS * HEAD_PAD
    in_specs = [
        tile(d),
        pl.BlockSpec((1, N_MOD, d), lambda i, b: (jnp.where(i == 0, bsz, b), 0, 0)),
        const(1, d),
        const(d, P_TOTAL),
        const(1, MLA_Q_RANK),
        const(1, MLA_KV_RANK),
        const(MLA_Q_RANK, 2 * nq),
        const(MLA_HEADS * MLA_NOPE, MLA_KV_RANK),
        const(MLA_KV_RANK, nq),
        const(2 * MLA_ROPE, d),
        pl.BlockSpec((TM, nq), lambda i, b: (i, 0)),
        pl.BlockSpec((TM, nq), lambda i, b: (i, 0)),
        pl.BlockSpec((MLA_ROPE, TM), lambda i, b: (0, i)),
        pl.BlockSpec((MLA_ROPE, TM), lambda i, b: (0, i)),
    ]
    head_rows = pl.BlockSpec((1, MLA_HEADS, TM, HEAD_PAD), lambda i, b: (b, 0, i, 0))
    out_specs = [
        head_rows,
        pl.BlockSpec((1, MLA_HEADS, 1, HEAD_PAD, TM), lambda i, b: (b, 0, i, 0, 0)),
        head_rows,
        tile(GDN_QKV), tile(GDN_HEADS * GDN_DV),
        tile(GLA_HEADS * GLA_DK), tile(GLA_HEADS * GLA_DK),
        tile(GLA_HEADS * GLA_DV), tile(GLA_HEADS * GLA_DV),
        tile(LANE),
    ]
    sd = jax.ShapeDtypeStruct
    out_shape = [
        sd((bsz, MLA_HEADS, seq, HEAD_PAD), BF16),
        sd((bsz, MLA_HEADS, nt, HEAD_PAD, TM), BF16),
        sd((bsz, MLA_HEADS, seq, HEAD_PAD), BF16),
        sd((bsz, seq, GDN_QKV), F32), sd((bsz, seq, GDN_HEADS * GDN_DV), F32),
        sd((bsz, seq, GLA_HEADS * GLA_DK), F32), sd((bsz, seq, GLA_HEADS * GLA_DK), F32),
        sd((bsz, seq, GLA_HEADS * GLA_DV), F32), sd((bsz, seq, GLA_HEADS * GLA_DV), F32),
        sd((bsz, seq, LANE), F32),
    ]
    return pl.pallas_call(
        _pre_mix_kernel, out_shape=out_shape, grid=(nt, bsz), in_specs=in_specs, out_specs=out_specs,
        compiler_params=_cparams(("arbitrary", "arbitrary")), name="pre_mix",
    )(x, mods, norm_g, wts["w_in"], wts["q_norm"], wts["kv_norm"], wts["w_uq"], wts["w_kT"], wts["w_v"],
      wts["w_krT"], tabs["cq"], tabs["sq"], tabs["ckT"], tabs["skT"])


def _flash_kernel(q_ref, kT_ref, v_ref, o_ref):
    i = pl.program_id(2)
    tq = q_ref.shape[2]
    n_latent = kT_ref.shape[2] - 1
    outs = []
    for hh in range(2):
        q = q_ref[0, hh]

        def step(j, carry, hh=hh, q=q):
            m, acc = carry
            s = _dot(q, kT_ref[0, hh, j])
            m_new = jnp.maximum(m, jnp.max(s, axis=-1, keepdims=True))
            p = jnp.exp2(s - m_new)
            alpha = jnp.exp2(m - m_new)
            vj = v_ref[0, hh, pl.ds(pl.multiple_of(j * TM, TM), TM), :]
            acc = alpha * acc + _dot(p.astype(BF16), vj)
            return m_new, acc

        init = (jnp.full((tq, 1), -1e30, F32), jnp.zeros((tq, HEAD_PAD), F32))
        carry = step(0, init)
        m, acc = lax.fori_loop(1, jnp.where(i > 0, n_latent + 1, 1), step, carry)
        outs.append(acc[:, :MLA_V] / acc[:, MLA_V:MLA_V + 1])
    o_ref[0] = jnp.concatenate(outs, axis=-1)


def mla_attention(q, kT, v):
    bsz, nh, seq, _ = q.shape
    nt = seq // TM
    return pl.pallas_call(
        _flash_kernel,
        out_shape=jax.ShapeDtypeStruct((bsz, seq, nh * MLA_V), F32),
        grid=(bsz, nh // 2, nt),
        in_specs=[pl.BlockSpec((1, 2, TM, HEAD_PAD), lambda b, h, i: (b, h, i, 0)),
                  pl.BlockSpec((1, 2, nt, HEAD_PAD, TM), lambda b, h, i: (b, h, 0, 0, 0)),
                  pl.BlockSpec((1, 2, seq, HEAD_PAD), lambda b, h, i: (b, h, 0, 0))],
        out_specs=pl.BlockSpec((1, TM, 2 * MLA_V), lambda b, h, i: (b, i, h)),
        compiler_params=_cparams(("arbitrary", "arbitrary", "arbitrary")),
        name="mla_attention",
    )(q, kT, v)


def _group_mean_sq(x, gmat):
    xsq = x * x
    hi = xsq.astype(BF16)
    lo = (xsq - hi.astype(F32)).astype(BF16)
    return _dot(hi, gmat) + _dot(lo, gmat)


def _post_mix_kernel(x_ref, mod_ref, mla_ref, gdn_ref, gz_ref, gla_ref, lg_ref, gmat_ref,
                     gdn_g_ref, gla_g_ref, w_out_ref, post_g_ref, o_ref):
    gmat = gmat_ref[...]
    gdn = gdn_ref[0]
    gdn = gdn * lax.rsqrt(_group_mean_sq(gdn, gmat) + NORM_EPS) * gdn_g_ref[...] * _silu(gz_ref[0])
    gla = gla_ref[0]
    gla = gla * lax.rsqrt(_group_mean_sq(gla, gmat) + NORM_EPS) * gla_g_ref[...] * _silu(lg_ref[0])
    merged = jnp.concatenate([mla_ref[0], gdn, gla], axis=-1).astype(BF16)
    y = _dot(merged, w_out_ref[...])
    gate = mod_ref[0, 2:3, :]
    o_ref[0] = x_ref[0] + gate * _rms(y, post_g_ref[...])


def post_mix(x, mods, mla_o, gdn_o, gz, gla_o, lg, wts, post_g):
    bsz, seq, d = x.shape
    nt = seq // TM
    const = lambda *shape: pl.BlockSpec(shape, lambda b, i: (0,) * len(shape))
    tile = lambda w: pl.BlockSpec((1, TM, w), lambda b, i: (b, i, 0))
    hw = GDN_HEADS * GDN_DV
    return pl.pallas_call(
        _post_mix_kernel,
        out_shape=jax.ShapeDtypeStruct(x.shape, F32),
        grid=(bsz, nt),
        in_specs=[tile(d),
                  pl.BlockSpec((1, N_MOD, d), lambda b, i: (jnp.where(i == 0, bsz, b), 0, 0)),
                  tile(MLA_HEADS * MLA_V), tile(hw), tile(hw), tile(hw), tile(hw),
                  const(hw, hw), const(1, hw), const(1, hw), const(D_MIX, d), const(1, d)],
        out_specs=tile(d),
        compiler_params=_cparams(("arbitrary", "arbitrary")),
        name="post_mix",
    )(x, mods, mla_o, gdn_o, gz, gla_o, lg, wts["gmat"], wts["gdn_norm"], wts["gla_norm"], wts["w_out"], post_g)


def _conv3(z, w_ref, cols, prev_ok, next_ok):
    rows = z.shape[0]
    r = lax.broadcasted_iota(jnp.int32, (rows, 1), 0)
    z = jnp.where((r == HALO - 1) & jnp.logical_not(prev_ok), 0.0, z)
    z = jnp.where((r == HALO + TM) & jnp.logical_not(next_ok), 0.0, z)
    zm = pltpu.roll(z, 1, axis=0)[HALO:HALO + TM]
    zp = pltpu.roll(z, rows - 1, axis=0)[HALO:HALO + TM]
    zc = z[HALO:HALO + TM]
    return zm * w_ref[0:1, cols] + zc * w_ref[1:2, cols] + zp * w_ref[2:3, cols]


def _ffn_kernel(x_ref, xp_ref, xn_ref, mod_ref, pre_g_ref, w_in_ref, b_in_ref, cw_ref, cb_ref,
                w_out_ref, post_g_ref, o_ref):
    i = pl.program_id(1)
    nt = pl.num_programs(1)
    prev_ok = i > 1
    next_ok = (i > 0) & (i < nt - 1)
    x = x_ref[0]
    xx = jnp.concatenate([xp_ref[0], x, xn_ref[0]], axis=0)
    shift = mod_ref[0, 3:4, :]
    scale = mod_ref[0, 4:5, :]
    hb = (_rms(xx, pre_g_ref[...]) * (1.0 + scale) + shift).astype(BF16)
    acc = jnp.zeros((TM, x.shape[1]), F32)
    for j in range(FFN_HIDDEN // FFN_CHUNK):
        ca = slice(j * FFN_CHUNK, (j + 1) * FFN_CHUNK)
        cg = slice(FFN_HIDDEN + j * FFN_CHUNK, FFN_HIDDEN + (j + 1) * FFN_CHUNK)
        za = _dot(hb, w_in_ref[:, ca]) + b_in_ref[:, ca]
        zg = _dot(hb, w_in_ref[:, cg]) + b_in_ref[:, cg]
        a = _conv3(za, cw_ref, ca, prev_ok, next_ok) + cb_ref[:, ca]
        g = _conv3(zg, cw_ref, cg, prev_ok, next_ok) + cb_ref[:, cg]
        act = (a * _silu(g)).astype(BF16)
        acc = acc + _dot(act, w_out_ref[ca, :])
    gate = mod_ref[0, 5:6, :]
    o_ref[0] = x + gate * _rms(acc, post_g_ref[...])


def conv_ffn(x, mods, pre_g, wts, post_g):
    bsz, seq, d = x.shape
    nt = seq // TM
    nb = seq // HALO
    per = TM // HALO
    const = lambda *shape: pl.BlockSpec(shape, lambda b, i: (0,) * len(shape))
    tile = pl.BlockSpec((1, TM, d), lambda b, i: (b, i, 0))
    return pl.pallas_call(
        _ffn_kernel,
        out_shape=jax.ShapeDtypeStruct(x.shape, F32),
        grid=(bsz, nt),
        in_specs=[tile,
                  pl.BlockSpec((1, HALO, d), lambda b, i: (b, jnp.maximum(i * per - 1, 0), 0)),
                  pl.BlockSpec((1, HALO, d), lambda b, i: (b, jnp.minimum((i + 1) * per, nb - 1), 0)),
                  pl.BlockSpec((1, N_MOD, d), lambda b, i: (jnp.where(i == 0, bsz, b), 0, 0)),
                  const(1, d), const(d, 2 * FFN_HIDDEN), const(1, 2 * FFN_HIDDEN),
                  const(3, 2 * FFN_HIDDEN), const(1, 2 * FFN_HIDDEN), const(FFN_HIDDEN, d), const(1, d)],
        out_specs=tile,
        compiler_params=_cparams(("arbitrary", "arbitrary")),
        name="conv_ffn",
    )(x, x, x, mods, pre_g, wts["ffn_w_in"], wts["ffn_b_in"], wts["ffn_conv_w"], wts["ffn_conv_b"],
      wts["ffn_w_out"], post_g)


def _rope_tables(t_latent):
    n = MLA_ROPE // 4
    t = jnp.arange(t_latent)
    row = (t // GRID_W).astype(F32)
    col = (t % GRID_W).astype(F32)
    inv = ROPE_THETA ** (-jnp.arange(n, dtype=F32) / n)
    ang = jnp.stack([row[:, None] * inv, col[:, None] * inv], axis=1)
    cos, sin = jnp.cos(ang), jnp.sin(ang)
    c32 = jnp.stack([cos, cos], axis=2).reshape(t_latent, MLA_ROPE)
    s32 = jnp.stack([-sin, sin], axis=2).reshape(t_latent, MLA_ROPE)
    c32 = jnp.concatenate([jnp.ones((CTX_LEN, MLA_ROPE), F32), c32], axis=0)
    s32 = jnp.concatenate([jnp.zeros((CTX_LEN, MLA_ROPE), F32), s32], axis=0)
    seq = CTX_LEN + t_latent
    qscale = (MLA_NOPE + MLA_ROPE) ** -0.5 * math.log2(math.e)
    pad = HEAD_PAD - MLA_NOPE - MLA_ROPE
    cq = jnp.concatenate([jnp.ones((seq, MLA_NOPE), F32), c32, jnp.zeros((seq, pad), F32)], axis=1) * qscale
    sq = jnp.concatenate([jnp.zeros((seq, MLA_NOPE), F32), s32, jnp.zeros((seq, pad), F32)], axis=1) * qscale
    return {"cq": jnp.tile(cq, (1, MLA_HEADS)), "sq": jnp.tile(sq, (1, MLA_HEADS)),
            "ckT": c32.T, "skT": s32.T}


_ROPE_PARTNER = np.arange(MLA_ROPE) ^ (MLA_ROPE // 4)


def _prep_layer(i, w_in, mla_q_norm, mla_w_uq, mla_kv_norm, mla_w_ukv, gdn_norm, gla_norm, w_out,
                ffn_w_in, ffn_b_in, ffn_conv_w, ffn_conv_b, ffn_w_out):
    o = IN_OFFS
    w = w_in[i]
    d = w.shape[0]
    piece = lambda k: w[:, o[k]:o[k + 1]]
    misc_pad = LANE - (MLA_ROPE + 4 * GDN_HEADS + 2 * GLA_GATE_RANK)
    w_in_p = jnp.concatenate(
        [piece(0), piece(1), piece(3), piece(4), piece(7), piece(8), piece(9), piece(10),
         piece(2), piece(5), piece(6), piece(11), jnp.zeros((d, misc_pad), F32)], axis=1).astype(BF16)
    kr = piece(2)
    w_krT = jnp.concatenate([kr, kr[:, _ROPE_PARTNER]], axis=1).T.astype(BF16)

    hq = MLA_NOPE + MLA_ROPE
    uq = mla_w_uq[i].reshape(MLA_Q_RANK, MLA_HEADS, hq)
    zq = lambda n: jnp.zeros((MLA_Q_RANK, MLA_HEADS, n), F32)
    plain = jnp.concatenate([uq, zq(HEAD_PAD - hq)], axis=2)
    partner = jnp.concatenate([zq(MLA_NOPE), uq[:, :, MLA_NOPE:][:, :, _ROPE_PARTNER], zq(HEAD_PAD - hq)], axis=2)
    w_uq = jnp.concatenate([plain.reshape(MLA_Q_RANK, -1), partner.reshape(MLA_Q_RANK, -1)], axis=1).astype(BF16)

    ukv = mla_w_ukv[i].reshape(MLA_KV_RANK, MLA_HEADS, MLA_NOPE + MLA_V)
    w_kT = ukv[:, :, :MLA_NOPE].reshape(MLA_KV_RANK, -1).T.astype(BF16)
    w_v = jnp.concatenate([ukv[:, :, MLA_NOPE:], jnp.zeros((MLA_KV_RANK, MLA_HEADS, HEAD_PAD - MLA_V), F32)],
                          axis=2).reshape(MLA_KV_RANK, -1).astype(BF16)

    hw = GDN_HEADS * GDN_DV
    lane_head = np.arange(hw) // GDN_DV
    gmat = jnp.asarray((lane_head[:, None] == lane_head[None, :]).astype(np.float32) / GDN_DV).astype(BF16)
    return {
        "w_in": w_in_p, "w_krT": w_krT, "w_uq": w_uq, "w_kT": w_kT, "w_v": w_v,
        "q_norm": mla_q_norm[i][None], "kv_norm": mla_kv_norm[i][None],
        "gmat": gmat, "gdn_norm": jnp.tile(gdn_norm[i], GDN_HEADS)[None],
        "gla_norm": jnp.tile(gla_norm[i], GLA_HEADS)[None], "w_out": w_out[i].astype(BF16),
        "ffn_w_in": ffn_w_in[i].astype(BF16), "ffn_b_in": ffn_b_in[i][None], "ffn_conv_w": ffn_conv_w[i],
        "ffn_conv_b": ffn_conv_b[i][None], "ffn_w_out": ffn_w_out[i].astype(BF16),
    }


def _l2norm(x):
    return x * lax.rsqrt(jnp.sum(x * x, axis=-1, keepdims=True) + NORM_EPS)


def _dwconv(x, w):
    return lax.conv_general_dilated(
        x, w[:, None, :], window_strides=(1,), padding=[(1, 1)],
        dimension_numbers=('NWC', 'WIO', 'NWC'), feature_group_count=x.shape[-1])


def _gdn_chunked(q, k, v, g, beta, S0):
    B, H, T, Dk = q.shape
    C = GDN_CHUNK
    N = T // C
    f = lambda a: a.reshape(B, H, N, C, *a.shape[3:])
    q, k, v, g, beta = f(q) * Dk ** -0.5, f(k), f(v), f(g), f(beta)
    gc = jnp.cumsum(g, axis=-1)
    incl = jnp.tril(jnp.ones((C, C), bool))
    strict = jnp.tril(jnp.ones((C, C), bool), -1)
    diff = gc[..., :, None] - gc[..., None, :]
    decay = jnp.where(incl, jnp.exp(jnp.minimum(diff, 0.0)), 0.0)
    kb = k * beta[..., None]
    L = jnp.where(strict, jnp.einsum('bhnik,bhnjk->bhnij', kb, k) * decay, 0.0)
    Tinv = jnp.eye(C, dtype=F32) - L
    P = L
    for _ in range(int(math.log2(C)) - 1):
        P = P @ P
        Tinv = Tinv + Tinv @ P
    u = Tinv @ (v * beta[..., None])
    w = Tinv @ (kb * jnp.exp(gc)[..., None])
    A = jnp.einsum('bhnik,bhnjk->bhnij', q, k) * decay
    qd = q * jnp.exp(gc)[..., None]
    kend = k * jnp.exp(gc[..., -1:] - gc)[..., None]
    glast = jnp.exp(gc[..., -1])

    def step(S, xs):
        qd_, w_, u_, A_, kend_, glast_ = xs
        v_new = u_ - jnp.einsum('bhck,bhkv->bhcv', w_, S)
        o = jnp.einsum('bhck,bhkv->bhcv', qd_, S) + jnp.einsum('bhij,bhjv->bhiv', A_, v_new)
        S = S * glast_[..., None, None] + jnp.einsum('bhck,bhcv->bhkv', kend_, v_new)
        return S, o

    xs = tuple(jnp.moveaxis(a, 2, 0) for a in (qd, w, u, A, kend, glast))
    S, o = lax.scan(step, S0, xs)
    return jnp.moveaxis(o, 0, 2).reshape(B, H, T, -1), S


def _gla_chunked(q, k, v, la, S0):
    B, H, T, Dk = q.shape
    C = GLA_CHUNK
    N = T // C
    f = lambda a: a.reshape(B, H, N, C, *a.shape[3:])
    q, k, v, la = f(q) * Dk ** -0.5, f(k), f(v), f(la)
    b = jnp.cumsum(la, axis=3)
    incl = jnp.tril(jnp.ones((C, C), bool))[:, :, None]
    diff = b[:, :, :, :, None, :] - b[:, :, :, None, :, :]
    decay = jnp.where(incl, jnp.exp(jnp.minimum(diff, 0.0)), 0.0)
    A = jnp.einsum('bhnik,bhnjk,bhnijk->bhnij', q, k, decay)
    o_intra = jnp.einsum('bhnij,bhnjv->bhniv', A, v)
    qd = q * jnp.exp(b)
    kend = k * jnp.exp(b[..., -1:, :] - b)
    glast = jnp.exp(b[..., -1, :])

    def step(S, xs):
        qd_, kend_, v_, glast_ = xs
        o = jnp.einsum('bhck,bhkv->bhcv', qd_, S)
        S = S * glast_[..., :, None] + jnp.einsum('bhck,bhcv->bhkv', kend_, v_)
        return S, o

    xs = tuple(jnp.moveaxis(a, 2, 0) for a in (qd, kend, v, glast))
    S, o_inter = lax.scan(step, S0, xs)
    o = o_intra + jnp.moveaxis(o_inter, 0, 2)
    return o.reshape(B, H, T, -1), S


def _rev(a, d):
    return jnp.flip(a, axis=2) if d else a


def _bidir(chunk_fn, seq_c, dir_c, seq_x, dir_x, state_shape):
    outs_c, outs_x = [], []
    for d in range(2):
        S0 = jnp.zeros(state_shape, F32)
        o_c, S_c = chunk_fn(*[_rev(a, d) for a in seq_c], *[_rev(a[d], d) for a in dir_c], S0)
        o_x, _ = chunk_fn(*[_rev(a, d) for a in seq_x], *[_rev(a[d], d) for a in dir_x], S_c)
        outs_c.append(_rev(o_c, d))
        outs_x.append(_rev(o_x, d))
    return outs_c[0] + outs_c[1], outs_x[0] + outs_x[1]


def _jax_mixers(gqkv, misc, lq, lk, lv, conv_w, a_log, dt_bias, w_gk, b_gk):
    bsz = gqkv.shape[0]

    def gdn_feats(qkv_pre, ab):
        qkv = _silu(_dwconv(qkv_pre, conv_w))
        B, L = qkv.shape[:2]
        q, k, v = jnp.split(qkv, [GDN_HEADS * GDN_DK, 2 * GDN_HEADS * GDN_DK], axis=-1)
        heads = lambda a, d: jnp.moveaxis(a.reshape(B, L, GDN_HEADS, d), 2, 1)
        q, k, v = _l2norm(heads(q, GDN_DK)), _l2norm(heads(k, GDN_DK)), heads(v, GDN_DV)
        a = ab[..., :2 * GDN_HEADS].reshape(B, L, 2, GDN_HEADS)
        bb = ab[..., 2 * GDN_HEADS:].reshape(B, L, 2, GDN_HEADS)
        g = -jnp.exp(a_log) * jax.nn.softplus(a + dt_bias)
        beta = jax.nn.sigmoid(bb)
        return (q, k, v), (jnp.transpose(g, (2, 0, 3, 1)), jnp.transpose(beta, (2, 0, 3, 1)))

    def gla_feats(q, k, v, lr):
        B, L = q.shape[:2]
        heads = lambda a, d: jnp.moveaxis(a.reshape(B, L, GLA_HEADS, d), 2, 1)
        lr = lr.reshape(B, L, 2, GLA_GATE_RANK)
        gk = jnp.einsum('bldr,drk->dblk', lr, w_gk) + b_gk[:, None, None, :]
        la = jax.nn.log_sigmoid(gk) / GLA_NORMALIZER
        la = jnp.moveaxis(la.reshape(2, B, L, GLA_HEADS, GLA_DK), 3, 2)
        return (heads(q, GLA_DK), heads(k, GLA_DK), heads(v, GLA_DV)), (la,)

    ab = misc[..., M_A:M_LR]
    lr = misc[..., M_LR:M_LR + 2 * GLA_GATE_RANK]
    c, x = (lambda a: a[:, :CTX_LEN]), (lambda a: a[:, CTX_LEN:])
    seq_c, dir_c = gdn_feats(c(gqkv), c(ab))
    seq_x, dir_x = gdn_feats(x(gqkv), x(ab))
    gdn_c, gdn_x = _bidir(_gdn_chunked, seq_c, dir_c, seq_x, dir_x, (bsz, GDN_HEADS, GDN_DK, GDN_DV))
    sq_c, dr_c = gla_feats(c(lq), c(lk), c(lv), c(lr))
    sq_x, dr_x = gla_feats(x(lq), x(lk), x(lv), x(lr))
    gla_c, gla_x = _bidir(_gla_chunked, sq_c, dr_c, sq_x, dr_x, (bsz, GLA_HEADS, GLA_DK, GLA_DV))
    flat = lambda oc, ox: jnp.moveaxis(jnp.concatenate([oc, ox], axis=2), 1, 2).reshape(bsz, -1, GDN_HEADS * GDN_DV)
    return flat(gdn_c, gdn_x), flat(gla_c, gla_x)


def kernel(x, c, ctx, c_ctx, w_ada, b_ada, norm_mix_pre, norm_mix_post, norm_ffn_pre, norm_ffn_post,
           w_in, mla_q_norm, mla_w_uq, mla_kv_norm, mla_w_ukv, gdn_conv_w, gdn_a_log, gdn_dt_bias,
           gdn_norm, gla_w_gk, gla_b_gk, gla_norm, w_out, ffn_w_in, ffn_b_in, ffn_conv_w, ffn_conv_b,
           ffn_w_out):
    bsz, t_latent, d = x.shape
    assert ctx.shape[1] == CTX_LEN and t_latent % TM == 0 and t_latent % GRID_W == 0
    depth = w_ada.shape[0]
    xs = jnp.concatenate([ctx, x], axis=1)
    mod_rows = -(-(bsz + 1) // SUBLANE) * SUBLANE
    cvec = jnp.concatenate([c, c_ctx[None], jnp.zeros((mod_rows - bsz - 1, d), F32)], axis=0)
    mods_all = ada_modulation(cvec, w_ada, b_ada).reshape(depth, mod_rows, N_MOD, d)
    tabs = _rope_tables(t_latent)
    for i in range(depth):
        mods = mods_all[i]
        wts = _prep_layer(i, w_in, mla_q_norm, mla_w_uq, mla_kv_norm, mla_w_ukv, gdn_norm, gla_norm, w_out,
                          ffn_w_in, ffn_b_in, ffn_conv_w, ffn_conv_b, ffn_w_out)
        q, kT, v, gqkv, gz, lq, lk, lv, lg, misc = pre_mix(xs, mods, norm_mix_pre[i][None], wts, tabs)
        mla_o = mla_attention(q, kT, v)
        gdn_o, gla_o = _jax_mixers(gqkv, misc, lq, lk, lv, gdn_conv_w[i], gdn_a_log[i], gdn_dt_bias[i],
                                   gla_w_gk[i], gla_b_gk[i])
        xs = post_mix(xs, mods, mla_o, gdn_o, gz, gla_o, lg, wts, norm_mix_post[i][None])
        xs = conv_ffn(xs, mods, norm_ffn_pre[i][None], wts, norm_ffn_post[i][None])
    return xs[:, CTX_LEN:]
```

```python
import functools
import math

import numpy as np
import jax
import jax.numpy as jnp
from jax import lax
from jax.experimental import pallas as pl
from jax.experimental.pallas import tpu as pltpu

F32 = jnp.float32
BF16 = jnp.bfloat16

D_MODEL = 1024
DEPTH = 4
GRID_W = 64
CTX_LEN = 256
N_MOD = 6
NORM_EPS = 1e-6

MLA_HEADS = 8
MLA_Q_RANK = 384
MLA_KV_RANK = 256
MLA_NOPE = 64
MLA_ROPE = 32
MLA_V = 64
ROPE_THETA = 10000.0

GDN_HEADS = 4
GDN_DK = 64
GDN_DV = 64
GDN_CHUNK = 64

GLA_HEADS = 4
GLA_DK = 32
GLA_DV = 64
GLA_GATE_RANK = 16
GLA_NORMALIZER = 16.0
GLA_CHUNK = 16

FFN_HIDDEN = 2560
D_MIX = MLA_HEADS * MLA_V + GDN_HEADS * GDN_DV + GLA_HEADS * GLA_DV
GDN_QKV = GDN_HEADS * (2 * GDN_DK + GDN_DV)
IN_SIZES = (MLA_Q_RANK, MLA_KV_RANK, MLA_ROPE,
            GDN_QKV, GDN_HEADS * GDN_DV, 2 * GDN_HEADS, 2 * GDN_HEADS,
            GLA_HEADS * GLA_DK, GLA_HEADS * GLA_DK, GLA_HEADS * GLA_DV, GLA_HEADS * GLA_DV,
            2 * GLA_GATE_RANK)
IN_OFFS = tuple(int(s) for s in np.cumsum((0,) + IN_SIZES))

LANE = 128
SUBLANE = 8
TM = CTX_LEN
TQ = 2 * TM
HALO = SUBLANE
HEAD_PAD = LANE
FFN_CHUNK = 512
RC = 64
assert RC == GDN_DK
VMEM_LIMIT = 56 * 1024 * 1024

P_CQ = 0
P_CKV = P_CQ + MLA_Q_RANK
P_GQKV = P_CKV + MLA_KV_RANK
P_GZ = P_GQKV + GDN_QKV
P_LQ = P_GZ + GDN_HEADS * GDN_DV
P_LK = P_LQ + GLA_HEADS * GLA_DK
P_LV = P_LK + GLA_HEADS * GLA_DK
P_LG = P_LV + GLA_HEADS * GLA_DV
P_MISC = P_LG + GLA_HEADS * GLA_DV
P_TOTAL = P_MISC + LANE
M_KR = 0
M_A = MLA_ROPE
M_B = M_A + 2 * GDN_HEADS
M_LR = M_B + 2 * GDN_HEADS


def _cparams(sem, vmem=VMEM_LIMIT):
    return pltpu.CompilerParams(dimension_semantics=sem, vmem_limit_bytes=vmem)


def _rms(x, g):
    return x * lax.rsqrt(jnp.mean(x * x, axis=-1, keepdims=True) + NORM_EPS) * g


def _silu(x):
    return x * jax.nn.sigmoid(x)


def _softplus(x):
    return jnp.maximum(x, 0.0) + jnp.log1p(jnp.exp(-jnp.abs(x)))


def _dot(a, b):
    return jnp.dot(a, b, preferred_element_type=F32)


def _dot_t(a, b):
    return lax.dot_general(a, b, (((1,), (1,)), ((), ())), preferred_element_type=F32)


def _ada_kernel(c_ref, w_ref, b_ref, o_ref):
    o_ref[0] = _dot(_silu(c_ref[...]), w_ref[0]) + b_ref[0]


def ada_modulation(cvec, w_ada, b_ada):
    depth, d, n = w_ada.shape
    rows = cvec.shape[0]
    tn = n // 4
    return pl.pallas_call(
        _ada_kernel,
        out_shape=jax.ShapeDtypeStruct((depth, rows, n), F32),
        grid=(depth, n // tn),
        in_specs=[pl.BlockSpec((rows, d), lambda l, j: (0, 0)),
                  pl.BlockSpec((1, d, tn), lambda l, j: (l, 0, j)),
                  pl.BlockSpec((1, 1, tn), lambda l, j: (l, 0, j))],
        out_specs=pl.BlockSpec((1, rows, tn), lambda l, j: (l, 0, j)),
        compiler_params=_cparams(("arbitrary", "arbitrary")),
        name="ada_modulation",
    )(cvec, w_ada, b_ada.reshape(depth, 1, n))


def _pre_mix_kernel(x_ref, mod_ref, g_ref, w_in_ref, qn_ref, kvn_ref, w_uq_ref, w_kT_ref, w_v_ref,
                    w_krT_ref, cq_ref, sq_ref, ckT_ref, skT_ref,
                    q_ref, kT_ref, v_ref, gqkv_ref, gz_ref, lq_ref, lk_ref, lv_ref, lg_ref, misc_ref):
    x = x_ref[0]
    shift = mod_ref[0, 0:1, :]
    scale = mod_ref[0, 1:2, :]
    h = _rms(x, g_ref[...]) * (1.0 + scale) + shift
    hb = h.astype(BF16)
    p = _dot(hb, w_in_ref[...])

    gqkv_ref[0] = p[:, P_GQKV:P_GZ]
    gz_ref[0] = p[:, P_GZ:P_LQ]
    lq_ref[0] = p[:, P_LQ:P_LK]
    lk_ref[0] = p[:, P_LK:P_LV]
    lv_ref[0] = p[:, P_LV:P_LG]
    lg_ref[0] = p[:, P_LG:P_MISC]
    misc_ref[0] = p[:, P_MISC:P_TOTAL]

    cqn = _rms(p[:, P_CQ:P_CKV], qn_ref[...]).astype(BF16)
    ckvn = _rms(p[:, P_CKV:P_GQKV], kvn_ref[...]).astype(BF16)
    nq = MLA_HEADS * HEAD_PAD
    qf = _dot(cqn, w_uq_ref[...])
    q = qf[:, :nq] * cq_ref[...] + qf[:, nq:] * sq_ref[...]
    vf = _dot(ckvn, w_v_ref[...])
    one_lane = lax.broadcasted_iota(jnp.int32, (1, HEAD_PAD), 1) == MLA_V
    kTn = _dot_t(w_kT_ref[...], ckvn)
    krT = _dot_t(w_krT_ref[...], hb)
    kr = (krT[:MLA_ROPE] * ckT_ref[...] + krT[MLA_ROPE:] * skT_ref[...]).astype(BF16)
    zpad = jnp.zeros((HEAD_PAD - MLA_NOPE - MLA_ROPE, kr.shape[1]), BF16)
    for hd in range(MLA_HEADS):
        q_ref[0, hd] = q[:, hd * HEAD_PAD:(hd + 1) * HEAD_PAD].astype(BF16)
        vh = vf[:, hd * HEAD_PAD:(hd + 1) * HEAD_PAD]
        v_ref[0, hd] = jnp.where(one_lane, 1.0, vh).astype(BF16)
        kT_ref[0, hd, 0, 0:MLA_NOPE, :] = kTn[hd * MLA_NOPE:(hd + 1) * MLA_NOPE].astype(BF16)
        kT_ref[0, hd, 0, MLA_NOPE:MLA_NOPE + MLA_ROPE, :] = kr
        kT_ref[0, hd, 0, MLA_NOPE + MLA_ROPE:, :] = zpad


def pre_mix(x, mods, norm_g, wts, tabs):
    bsz, seq, d = x.shape
    nt = seq // TM
    const = lambda *shape: pl.BlockSpec(shape, lambda i, b: (0,) * len(shape))
    tile = lambda w: pl.BlockSpec((1, TM, w), lambda i, b: (b, i, 0))
    nq = MLA_HEADS * HEAD_PAD
    in_specs = [
        tile(d),
        pl.BlockSpec((1, N_MOD, d), lambda i, b: (jnp.where(i == nt - 1, bsz, b), 0, 0)),
        const(1, d),
        const(d, P_TOTAL),
        const(1, MLA_Q_RANK),
        const(1, MLA_KV_RANK),
        const(MLA_Q_RANK, 2 * nq),
        const(MLA_HEADS * MLA_NOPE, MLA_KV_RANK),
        const(MLA_KV_RANK, nq),
        const(2 * MLA_ROPE, d),
        pl.BlockSpec((TM, nq), lambda i, b: (i, 0)),
        pl.BlockSpec((TM, nq), lambda i, b: (i, 0)),
        pl.BlockSpec((MLA_ROPE, TM), lambda i, b: (0, i)),
        pl.BlockSpec((MLA_ROPE, TM), lambda i, b: (0, i)),
    ]
    head_rows = pl.BlockSpec((1, MLA_HEADS, TM, HEAD_PAD), lambda i, b: (b, 0, i, 0))
    out_specs = [
        head_rows,
        pl.BlockSpec((1, MLA_HEADS, 1, HEAD_PAD, TM), lambda i, b: (b, 0, i, 0, 0)),
        head_rows,
        tile(GDN_QKV), tile(GDN_HEADS * GDN_DV),
        tile(GLA_HEADS * GLA_DK), tile(GLA_HEADS * GLA_DK),
        tile(GLA_HEADS * GLA_DV), tile(GLA_HEADS * GLA_DV),
        tile(LANE),
    ]
    sd = jax.ShapeDtypeStruct
    out_shape = [
        sd((bsz, MLA_HEADS, seq, HEAD_PAD), BF16),
        sd((bsz, MLA_HEADS, nt, HEAD_PAD, TM), BF16),
        sd((bsz, MLA_HEADS, seq, HEAD_PAD), BF16),
        sd((bsz, seq, GDN_QKV), F32), sd((bsz, seq, GDN_HEADS * GDN_DV), F32),
        sd((bsz, seq, GLA_HEADS * GLA_DK), F32), sd((bsz, seq, GLA_HEADS * GLA_DK), F32),
        sd((bsz, seq, GLA_HEADS * GLA_DV), F32), sd((bsz, seq, GLA_HEADS * GLA_DV), F32),
        sd((bsz, seq, LANE), F32),
    ]
    return pl.pallas_call(
        _pre_mix_kernel, out_shape=out_shape, grid=(nt, bsz), in_specs=in_specs, out_specs=out_specs,
        compiler_params=_cparams(("arbitrary", "arbitrary")), name="pre_mix",
    )(x, mods, norm_g, wts["w_in"], wts["q_norm"], wts["kv_norm"], wts["w_uq"], wts["w_kT"], wts["w_v"],
      wts["w_krT"], tabs["cq"], tabs["sq"], tabs["ckT"], tabs["skT"])


def _flash_kernel(q_ref, kT_ref, v_ref, *rest, n_pairs):
    o_ref = rest[-1]
    tq = q_ref.shape[2]
    ctx = kT_ref.shape[2] - 1
    qs = [q_ref[0, hh] for hh in range(2)]

    def attend(carry, kts, vs):
        out = []
        for hh in range(2):
            m, acc = carry[2 * hh], carry[2 * hh + 1]
            s = _dot(qs[hh], kts[hh])
            m_new = jnp.maximum(m, jnp.max(s, axis=-1, keepdims=True))
            p = jnp.exp2(s - m_new).astype(BF16)
            alpha = jnp.exp2(m - m_new)
            out += [m_new, alpha * acc + _dot(p, vs[hh])]
        return tuple(out)

    init = (jnp.full((tq, 1), -1e30, F32), jnp.zeros((tq, HEAD_PAD), F32)) * 2
    carry = attend(init, [kT_ref[0, hh, ctx] for hh in range(2)],
                   [v_ref[0, hh, ctx * TM:(ctx + 1) * TM, :] for hh in range(2)])

    def body(j, carry):
        kts = [jnp.concatenate([kT_ref[0, hh, 2 * j], kT_ref[0, hh, 2 * j + 1]], axis=1) for hh in range(2)]
        rows = pl.ds(pl.multiple_of(j * 2 * TM, 2 * TM), 2 * TM)
        return attend(carry, kts, [v_ref[0, hh, rows, :] for hh in range(2)])

    if n_pairs:
        carry = lax.fori_loop(0, n_pairs, body, carry)
    o_ref[0] = jnp.concatenate([carry[2 * hh + 1][:, :MLA_V] / carry[2 * hh + 1][:, MLA_V:MLA_V + 1]
                                for hh in range(2)], axis=-1)


def mla_attention(q, kT, v):
    bsz, nh, seq, _ = q.shape
    nt = seq // TM
    t_latent = seq - CTX_LEN
    assert t_latent % TQ == 0 and (nt - 1) % 2 == 0
    kv_specs = [pl.BlockSpec((1, 2, nt, HEAD_PAD, TM), lambda b, h, i: (b, h, 0, 0, 0)),
                pl.BlockSpec((1, 2, seq, HEAD_PAD), lambda b, h, i: (b, h, 0, 0))]
    sem = ("arbitrary", "arbitrary", "arbitrary")
    out_shape = jax.ShapeDtypeStruct((bsz, seq, nh * MLA_V), F32)
    o = pl.pallas_call(
        functools.partial(_flash_kernel, n_pairs=(nt - 1) // 2),
        out_shape=out_shape,
        grid=(bsz, nh // 2, t_latent // TQ),
        in_specs=[pl.BlockSpec((1, 2, TQ, HEAD_PAD), lambda b, h, i: (b, h, i, 0))] + kv_specs,
        out_specs=pl.BlockSpec((1, TQ, 2 * MLA_V), lambda b, h, i: (b, i, h)),
        compiler_params=_cparams(sem), name="mla_attention_latent",
    )(q, kT, v)
    return pl.pallas_call(
        functools.partial(_flash_kernel, n_pairs=0),
        out_shape=out_shape,
        grid=(bsz, nh // 2, 1),
        in_specs=[pl.BlockSpec((1, 2, TM, HEAD_PAD), lambda b, h, i: (b, h, nt - 1, 0)),
                  pl.BlockSpec((1, 2, 1, HEAD_PAD, TM), lambda b, h, i: (b, h, nt - 1, 0, 0)),
                  pl.BlockSpec((1, 2, TM, HEAD_PAD), lambda b, h, i: (b, h, nt - 1, 0)),
                  pl.BlockSpec(memory_space=pl.ANY)],
        out_specs=pl.BlockSpec((1, TM, 2 * MLA_V), lambda b, h, i: (b, nt - 1, h)),
        input_output_aliases={3: 0},
        compiler_params=_cparams(sem), name="mla_attention_context",
    )(q, kT, v, o)


def _split3(x):
    h1 = x.astype(BF16)
    r1 = x - h1.astype(F32)
    h2 = r1.astype(BF16)
    h3 = (r1 - h2.astype(F32)).astype(BF16)
    return h1, h2, h3


def _sel_dot(sel, parts):
    return _dot(sel, parts[0]) + _dot(sel, parts[1]) + _dot(sel, parts[2])


def _dot_ta(a, b):
    return lax.dot_general(a, b, (((0,), (0,)), ((), ())), preferred_element_type=F32)


def _tri(n, d):
    r = lax.broadcasted_iota(jnp.int32, (n, n), 0)
    c = lax.broadcasted_iota(jnp.int32, (n, n), 1)
    return jnp.where(c <= r if d == 0 else c >= r, 1.0, 0.0).astype(BF16)


def _lane_head(width, per_head):
    return lax.broadcasted_iota(jnp.int32, (1, width), 1) // per_head


def _stack_heads(x, n_heads, per_head):
    lh = _lane_head(x.shape[1], per_head)
    return jnp.concatenate([jnp.where(lh == h, x, 0.0) for h in range(n_heads)], axis=0)


def _per_head_apply(mats, rhs, n_heads, per_head):
    n = mats.shape[1]
    lh = _lane_head(rhs.shape[1], per_head)
    out = jnp.zeros((n, rhs.shape[1]), F32)
    for h in range(n_heads):
        out = out + jnp.where(lh % n_heads == h, _dot(mats[h * n:(h + 1) * n], rhs), 0.0)
    return out


def _chunk_maps(n_chunks, n_latent):
    fwd = lambda n: (n + n_latent) % n_chunks
    bwd = lambda n: n_chunks - 1 - n
    return fwd, bwd


def _job_specs(widths, n_chunks, n_latent, seq):
    per = RC // HALO
    nb = seq // HALO
    specs = []
    for cmap in _chunk_maps(n_chunks, n_latent):
        chunk = lambda w, cmap=cmap: pl.BlockSpec((1, RC, w), lambda b, n: (b, cmap(n), 0))
        if widths[0] is not None:
            w0 = widths[0]
            specs += [chunk(w0),
                      pl.BlockSpec((1, HALO, w0), lambda b, n, cmap=cmap: (b, jnp.maximum(cmap(n) * per - 1, 0), 0)),
                      pl.BlockSpec((1, HALO, w0), lambda b, n, cmap=cmap: (b, jnp.minimum((cmap(n) + 1) * per, nb - 1), 0))]
        specs += [chunk(w) for w in widths[1:]]
    return specs


def _gdn_job(d, c, n_chunks, n_latent, qkv_ref, prev_ref, next_ref, misc_ref, conv_w_ref, esel_ref,
             alog_ref, dtb_ref, gsum_ref, s_ref, o_ref):
    hw = GDN_HEADS * GDN_DK
    prev_ok = (c != 0) & (c != n_latent)
    next_ok = (c != n_latent - 1) & (c != n_chunks - 1)
    z = jnp.concatenate([prev_ref[0], qkv_ref[0], next_ref[0]], axis=0)
    x = _silu(_conv3(z, conv_w_ref, slice(None), prev_ok, next_ok))
    gsum = gsum_ref[...]

    def l2n(a):
        sq = a * a
        hi = sq.astype(BF16)
        lo = (sq - hi.astype(F32)).astype(BF16)
        return a * lax.rsqrt(_dot(hi, gsum) + _dot(lo, gsum) + NORM_EPS)

    q = l2n(x[:, :hw]) * GDN_DK ** -0.5
    k = l2n(x[:, hw:2 * hw])
    v = x[:, 2 * hw:]

    ab = _sel_dot_r(_split3(misc_ref[0]), esel_ref[d])
    g = -jnp.exp(alog_ref[d]) * _softplus(ab[:, :hw] + dtb_ref[d])
    beta = jax.nn.sigmoid(ab[:, hw:])
    gc = _sel_dot(_tri(RC, d), _split3(g))
    total = gc[RC - 1:RC] if d == 0 else gc[0:1]
    eg = jnp.exp(gc)
    kb = k * beta
    rhs = jnp.concatenate([v * beta, kb * eg], axis=1)
    qd = q * eg
    kend = k * jnp.exp(total - gc)

    kk = _dot_t(_stack_heads(kb, GDN_HEADS, GDN_DK).astype(BF16), k.astype(BF16))
    qk = _dot_t(_stack_heads(q, GDN_HEADS, GDN_DK).astype(BF16), k.astype(BF16))
    grow = _transpose(gc)
    gcol = jnp.concatenate([gc[:, h * GDN_DK:h * GDN_DK + RC] for h in range(GDN_HEADS)], axis=0)
    decay = jnp.exp(jnp.minimum(gcol - grow, 0.0))
    ii = lax.broadcasted_iota(jnp.int32, (GDN_HEADS * RC, RC), 0) % RC
    jj = lax.broadcasted_iota(jnp.int32, (GDN_HEADS * RC, RC), 1)
    strict = ii > jj if d == 0 else ii < jj
    lmat = jnp.where(strict, kk * decay, 0.0)
    amat = jnp.where(strict | (ii == jj), qk * decay, 0.0)

    eye = jnp.where(lax.broadcasted_iota(jnp.int32, (RC, RC), 0) == lax.broadcasted_iota(jnp.int32, (RC, RC), 1),
                    1.0, 0.0)
    tinv = []
    for h in range(GDN_HEADS):
        p = lmat[h * RC:(h + 1) * RC]
        t = eye - p
        for _ in range(int(math.log2(RC)) - 1):
            p = _dot(p, p)
            t = t + _dot(t, p)
        tinv.append(t)
    uw = _per_head_apply(jnp.concatenate(tinv, axis=0), rhs, GDN_HEADS, GDN_DK)
    u, w = uw[:, :hw], uw[:, hw:]

    s = s_ref[d]
    ws = _dot(jnp.concatenate([w, qd], axis=0), s)
    v_new = u - ws[:RC]
    o_ref[0] = ws[RC:] + _per_head_apply(amat, v_new, GDN_HEADS, GDN_DV)
    rh = lax.broadcasted_iota(jnp.int32, (hw, hw), 0) // GDN_DK
    ch = lax.broadcasted_iota(jnp.int32, (hw, hw), 1) // GDN_DV
    s_ref[d] = s * jnp.exp(total) + jnp.where(rh == ch, _dot_ta(kend, v_new), 0.0)


def _sel_dot_r(parts, sel):
    return _dot(parts[0], sel) + _dot(parts[1], sel) + _dot(parts[2], sel)


def _transpose(x):
    n, m = x.shape
    xp = jnp.concatenate([x, jnp.zeros((LANE - n, m), x.dtype)], axis=0) if n < LANE else x
    return xp.T[:, :n]


def _gdn_kernel(*refs, n_chunks, n_latent):
    (qkv_f, prev_f, next_f, misc_f, qkv_r, prev_r, next_r, misc_r,
     conv_w, esel, alog, dtb, gsum, o_f, o_r, s_ref) = refs
    n = pl.program_id(1)

    @pl.when(n == 0)
    def _():
        s_ref[...] = jnp.zeros_like(s_ref)

    fwd, bwd = _chunk_maps(n_chunks, n_latent)
    consts = (conv_w, esel, alog, dtb, gsum, s_ref)
    _gdn_job(0, fwd(n), n_chunks, n_latent, qkv_f, prev_f, next_f, misc_f, *consts, o_f)
    _gdn_job(1, bwd(n), n_chunks, n_latent, qkv_r, prev_r, next_r, misc_r, *consts, o_r)


def gdn_mixer(gqkv, misc, wts):
    bsz, seq, _ = gqkv.shape
    n_chunks = seq // RC
    n_latent = (seq - CTX_LEN) // RC
    hw = GDN_HEADS * GDN_DV
    const = lambda *shape: pl.BlockSpec(shape, lambda b, n: (0,) * len(shape))
    fwd, bwd = _chunk_maps(n_chunks, n_latent)
    out = jax.ShapeDtypeStruct((bsz, seq, hw), F32)
    return pl.pallas_call(
        functools.partial(_gdn_kernel, n_chunks=n_chunks, n_latent=n_latent),
        out_shape=[out, out],
        grid=(bsz, n_chunks),
        in_specs=_job_specs((GDN_QKV, LANE), n_chunks, n_latent, seq)
        + [const(3, GDN_QKV), const(2, LANE, 2 * hw), const(2, 1, hw), const(2, 1, hw), const(hw, hw)],
        out_specs=[pl.BlockSpec((1, RC, hw), lambda b, n: (b, fwd(n), 0)),
                   pl.BlockSpec((1, RC, hw), lambda b, n: (b, bwd(n), 0))],
        scratch_shapes=[pltpu.VMEM((2, hw, hw), F32)],
        compiler_params=_cparams(("arbitrary", "arbitrary")),
        name="gdn_mixer",
    )(gqkv, gqkv, gqkv, misc, gqkv, gqkv, gqkv, misc,
      wts["gdn_conv_w"], wts["gdn_esel"], wts["gdn_alog"], wts["gdn_dtb"], wts["gsum"])


def _gla_job(d, q_ref, k_ref, v_ref, misc_ref, wgk_ref, bgk_ref, st_ref, o_ref):
    hk = GLA_HEADS * GLA_DK
    q = q_ref[0] * GLA_DK ** -0.5
    k = k_ref[0]
    v = v_ref[0]
    gk = _dot(misc_ref[0].astype(BF16), wgk_ref[d]) + bgk_ref[d]
    la = -_softplus(-gk) * (1.0 / GLA_NORMALIZER)
    parts = _split3(_sel_dot(_tri(RC, d), _split3(la)))
    b = parts[0].astype(F32) + parts[1].astype(F32) + parts[2].astype(F32)
    blast = b[RC - 1:RC] if d == 0 else b[0:1]

    ii = lax.broadcasted_iota(jnp.int32, (GLA_HEADS * RC, RC), 0) % RC
    jj = lax.broadcasted_iota(jnp.int32, (GLA_HEADS * RC, RC), 1)
    tt = lax.broadcasted_iota(jnp.int32, (RC, RC), 0)
    tc = lax.broadcasted_iota(jnp.int32, (RC, RC), 1)
    kb16 = k.astype(BF16)
    amat = jnp.where(ii == jj, _dot_t(_stack_heads(q, GLA_HEADS, GLA_DK).astype(BF16), kb16), 0.0)
    for ls in range(int(math.log2(RC))):
        s = 1 << ls
        start = (tt >> (ls + 1)) << (ls + 1)
        mid = start + (s - 1 if d == 0 else s)
        ref = _sel_dot(jnp.where(tc == mid, 1.0, 0.0).astype(BF16), parts)
        qt = q * jnp.exp(jnp.minimum(b - ref, 0.0))
        kt = k * jnp.exp(jnp.minimum(ref - b, 0.0))
        same = (ii >> (ls + 1)) == (jj >> (ls + 1))
        i_hi = (ii & (2 * s - 1)) >= s
        j_hi = (jj & (2 * s - 1)) >= s
        pair = same & (i_hi & ~j_hi if d == 0 else ~i_hi & j_hi)
        amat = amat + jnp.where(pair, _dot_t(_stack_heads(qt, GLA_HEADS, GLA_DK).astype(BF16), kt.astype(BF16)), 0.0)

    st = st_ref[d]
    o_ref[0] = _per_head_apply(amat, v, GLA_HEADS, GLA_DV) + _dot_t(q * jnp.exp(b), st)
    kend = k * jnp.exp(blast - b)
    rh = lax.broadcasted_iota(jnp.int32, st.shape, 0) // GLA_DV
    ch = lax.broadcasted_iota(jnp.int32, st.shape, 1) // GLA_DK
    st_ref[d] = st * jnp.exp(blast) + jnp.where(rh == ch, _dot_ta(v, kend), 0.0)


def _gla_kernel(q_f, k_f, v_f, misc_f, q_r, k_r, v_r, misc_r, wgk, bgk, o_f, o_r, st_ref):
    @pl.when(pl.program_id(1) == 0)
    def _():
        st_ref[...] = jnp.zeros_like(st_ref)

    _gla_job(0, q_f, k_f, v_f, misc_f, wgk, bgk, st_ref, o_f)
    _gla_job(1, q_r, k_r, v_r, misc_r, wgk, bgk, st_ref, o_r)


def gla_mixer(lq, lk, lv, misc, wts):
    bsz, seq, hk = lq.shape
    hv = lv.shape[2]
    n_chunks = seq // RC
    n_latent = (seq - CTX_LEN) // RC
    const = lambda *shape: pl.BlockSpec(shape, lambda b, n: (0,) * len(shape))
    fwd, bwd = _chunk_maps(n_chunks, n_latent)
    out = jax.ShapeDtypeStruct((bsz, seq, hv), F32)
    return pl.pallas_call(
        _gla_kernel,
        out_shape=[out, out],
        grid=(bsz, n_chunks),
        in_specs=_job_specs((None, hk, hk, hv, LANE), n_chunks, n_latent, seq)
        + [const(2, LANE, hk), const(2, 1, hk)],
        out_specs=[pl.BlockSpec((1, RC, hv), lambda b, n: (b, fwd(n), 0)),
                   pl.BlockSpec((1, RC, hv), lambda b, n: (b, bwd(n), 0))],
        scratch_shapes=[pltpu.VMEM((2, hv, hk), F32)],
        compiler_params=_cparams(("arbitrary", "arbitrary")),
        name="gla_mixer",
    )(lq, lk, lv, misc, lq, lk, lv, misc, wts["gla_wgk"], wts["gla_bgk"])


def _group_mean_sq(x, gmat):
    xsq = x * x
    hi = xsq.astype(BF16)
    lo = (xsq - hi.astype(F32)).astype(BF16)
    return _dot(hi, gmat) + _dot(lo, gmat)


def _post_mix_kernel(x_ref, mod_ref, mla_ref, gdn_f_ref, gdn_r_ref, gz_ref, gla_f_ref, gla_r_ref, lg_ref,
                     gmat_ref, gdn_g_ref, gla_g_ref, w_out_ref, post_g_ref, o_ref):
    gmat = gmat_ref[...]
    gdn = gdn_f_ref[0] + gdn_r_ref[0]
    gdn = gdn * lax.rsqrt(_group_mean_sq(gdn, gmat) + NORM_EPS) * gdn_g_ref[...] * _silu(gz_ref[0])
    gla = gla_f_ref[0] + gla_r_ref[0]
    gla = gla * lax.rsqrt(_group_mean_sq(gla, gmat) + NORM_EPS) * gla_g_ref[...] * _silu(lg_ref[0])
    merged = jnp.concatenate([mla_ref[0], gdn, gla], axis=-1).astype(BF16)
    y = _dot(merged, w_out_ref[...])
    gate = mod_ref[0, 2:3, :]
    o_ref[0] = x_ref[0] + gate * _rms(y, post_g_ref[...])


def post_mix(x, mods, mla_o, gdn_f, gdn_r, gz, gla_f, gla_r, lg, wts, post_g):
    bsz, seq, d = x.shape
    nt = seq // TM
    const = lambda *shape: pl.BlockSpec(shape, lambda b, i: (0,) * len(shape))
    tile = lambda w: pl.BlockSpec((1, TM, w), lambda b, i: (b, i, 0))
    hw = GDN_HEADS * GDN_DV
    return pl.pallas_call(
        _post_mix_kernel,
        out_shape=jax.ShapeDtypeStruct(x.shape, F32),
        grid=(bsz, nt),
        in_specs=[tile(d),
                  pl.BlockSpec((1, N_MOD, d), lambda b, i: (jnp.where(i == nt - 1, bsz, b), 0, 0)),
                  tile(MLA_HEADS * MLA_V), tile(hw), tile(hw), tile(hw), tile(hw), tile(hw), tile(hw),
                  const(hw, hw), const(1, hw), const(1, hw), const(D_MIX, d), const(1, d)],
        out_specs=tile(d),
        compiler_params=_cparams(("arbitrary", "arbitrary")),
        name="post_mix",
    )(x, mods, mla_o, gdn_f, gdn_r, gz, gla_f, gla_r, lg, wts["gmat"], wts["gdn_norm"], wts["gla_norm"],
      wts["w_out"], post_g)


def _conv3(z, w_ref, cols, prev_ok, next_ok):
    rows = z.shape[0]
    n = rows - 2 * HALO
    r = lax.broadcasted_iota(jnp.int32, (rows, 1), 0)
    z = jnp.where((r == HALO - 1) & jnp.logical_not(prev_ok), 0.0, z)
    z = jnp.where((r == HALO + n) & jnp.logical_not(next_ok), 0.0, z)
    zm = pltpu.roll(z, 1, axis=0)[HALO:HALO + n]
    zp = pltpu.roll(z, rows - 1, axis=0)[HALO:HALO + n]
    zc = z[HALO:HALO + n]
    return zm * w_ref[0:1, cols] + zc * w_ref[1:2, cols] + zp * w_ref[2:3, cols]


def _ffn_kernel(x_ref, xp_ref, xn_ref, mod_ref, pre_g_ref, w_in_ref, b_in_ref, cw_ref, cb_ref,
                w_out_ref, post_g_ref, o_ref):
    i = pl.program_id(1)
    nt = pl.num_programs(1)
    prev_ok = (i > 0) & (i < nt - 1)
    next_ok = i < nt - 2
    x = x_ref[0]
    xx = jnp.concatenate([xp_ref[0], x, xn_ref[0]], axis=0)
    shift = mod_ref[0, 3:4, :]
    scale = mod_ref[0, 4:5, :]
    hb = (_rms(xx, pre_g_ref[...]) * (1.0 + scale) + shift).astype(BF16)
    acc = jnp.zeros((TM, x.shape[1]), F32)
    for j in range(FFN_HIDDEN // FFN_CHUNK):
        ca = slice(j * FFN_CHUNK, (j + 1) * FFN_CHUNK)
        cg = slice(FFN_HIDDEN + j * FFN_CHUNK, FFN_HIDDEN + (j + 1) * FFN_CHUNK)
        za = _dot(hb, w_in_ref[:, ca]) + b_in_ref[:, ca]
        zg = _dot(hb, w_in_ref[:, cg]) + b_in_ref[:, cg]
        a = _conv3(za, cw_ref, ca, prev_ok, next_ok) + cb_ref[:, ca]
        g = _conv3(zg, cw_ref, cg, prev_ok, next_ok) + cb_ref[:, cg]
        act = (a * _silu(g)).astype(BF16)
        acc = acc + _dot(act, w_out_ref[ca, :])
    gate = mod_ref[0, 5:6, :]
    o_ref[0] = x + gate * _rms(acc, post_g_ref[...])


def conv_ffn(x, mods, pre_g, wts, post_g):
    bsz, seq, d = x.shape
    nt = seq // TM
    nb = seq // HALO
    per = TM // HALO
    const = lambda *shape: pl.BlockSpec(shape, lambda b, i: (0,) * len(shape))
    tile = pl.BlockSpec((1, TM, d), lambda b, i: (b, i, 0))
    return pl.pallas_call(
        _ffn_kernel,
        out_shape=jax.ShapeDtypeStruct(x.shape, F32),
        grid=(bsz, nt),
        in_specs=[tile,
                  pl.BlockSpec((1, HALO, d), lambda b, i: (b, jnp.maximum(i * per - 1, 0), 0)),
                  pl.BlockSpec((1, HALO, d), lambda b, i: (b, jnp.minimum((i + 1) * per, nb - 1), 0)),
                  pl.BlockSpec((1, N_MOD, d), lambda b, i: (jnp.where(i == nt - 1, bsz, b), 0, 0)),
                  const(1, d), const(d, 2 * FFN_HIDDEN), const(1, 2 * FFN_HIDDEN),
                  const(3, 2 * FFN_HIDDEN), const(1, 2 * FFN_HIDDEN), const(FFN_HIDDEN, d), const(1, d)],
        out_specs=tile,
        compiler_params=_cparams(("arbitrary", "arbitrary")),
        name="conv_ffn",
    )(x, x, x, mods, pre_g, wts["ffn_w_in"], wts["ffn_b_in"], wts["ffn_conv_w"], wts["ffn_conv_b"],
      wts["ffn_w_out"], post_g)


def _rope_tables(t_latent):
    n = MLA_ROPE // 4
    t = jnp.arange(t_latent)
    row = (t // GRID_W).astype(F32)
    col = (t % GRID_W).astype(F32)
    inv = ROPE_THETA ** (-jnp.arange(n, dtype=F32) / n)
    ang = jnp.stack([row[:, None] * inv, col[:, None] * inv], axis=1)
    cos, sin = jnp.cos(ang), jnp.sin(ang)
    c32 = jnp.stack([cos, cos], axis=2).reshape(t_latent, MLA_ROPE)
    s32 = jnp.stack([-sin, sin], axis=2).reshape(t_latent, MLA_ROPE)
    c32 = jnp.concatenate([c32, jnp.ones((CTX_LEN, MLA_ROPE), F32)], axis=0)
    s32 = jnp.concatenate([s32, jnp.zeros((CTX_LEN, MLA_ROPE), F32)], axis=0)
    seq = CTX_LEN + t_latent
    qscale = (MLA_NOPE + MLA_ROPE) ** -0.5 * math.log2(math.e)
    pad = HEAD_PAD - MLA_NOPE - MLA_ROPE
    cq = jnp.concatenate([jnp.ones((seq, MLA_NOPE), F32), c32, jnp.zeros((seq, pad), F32)], axis=1) * qscale
    sq = jnp.concatenate([jnp.zeros((seq, MLA_NOPE), F32), s32, jnp.zeros((seq, pad), F32)], axis=1) * qscale
    return {"cq": jnp.tile(cq, (1, MLA_HEADS)), "sq": jnp.tile(sq, (1, MLA_HEADS)),
            "ckT": c32.T, "skT": s32.T}


_ROPE_PARTNER = np.arange(MLA_ROPE) ^ (MLA_ROPE // 4)


def _prep_layer(i, w_in, mla_q_norm, mla_w_uq, mla_kv_norm, mla_w_ukv, gdn_conv_w, gdn_a_log, gdn_dt_bias,
                gdn_norm, gla_w_gk, gla_b_gk, gla_norm, w_out, ffn_w_in, ffn_b_in, ffn_conv_w, ffn_conv_b,
                ffn_w_out):
    o = IN_OFFS
    w = w_in[i]
    d = w.shape[0]
    piece = lambda k: w[:, o[k]:o[k + 1]]
    misc_pad = LANE - (MLA_ROPE + 4 * GDN_HEADS + 2 * GLA_GATE_RANK)
    w_in_p = jnp.concatenate(
        [piece(0), piece(1), piece(3), piece(4), piece(7), piece(8), piece(9), piece(10),
         piece(2), piece(5), piece(6), piece(11), jnp.zeros((d, misc_pad), F32)], axis=1).astype(BF16)
    kr = piece(2)
    w_krT = jnp.concatenate([kr, kr[:, _ROPE_PARTNER]], axis=1).T.astype(BF16)

    hq = MLA_NOPE + MLA_ROPE
    uq = mla_w_uq[i].reshape(MLA_Q_RANK, MLA_HEADS, hq)
    zq = lambda n: jnp.zeros((MLA_Q_RANK, MLA_HEADS, n), F32)
    plain = jnp.concatenate([uq, zq(HEAD_PAD - hq)], axis=2)
    partner = jnp.concatenate([zq(MLA_NOPE), uq[:, :, MLA_NOPE:][:, :, _ROPE_PARTNER], zq(HEAD_PAD - hq)], axis=2)
    w_uq = jnp.concatenate([plain.reshape(MLA_Q_RANK, -1), partner.reshape(MLA_Q_RANK, -1)], axis=1).astype(BF16)

    ukv = mla_w_ukv[i].reshape(MLA_KV_RANK, MLA_HEADS, MLA_NOPE + MLA_V)
    w_kT = ukv[:, :, :MLA_NOPE].reshape(MLA_KV_RANK, -1).T.astype(BF16)
    w_v = jnp.concatenate([ukv[:, :, MLA_NOPE:], jnp.zeros((MLA_KV_RANK, MLA_HEADS, HEAD_PAD - MLA_V), F32)],
                          axis=2).reshape(MLA_KV_RANK, -1).astype(BF16)

    hw = GDN_HEADS * GDN_DV
    lane_head = np.arange(hw) // GDN_DV
    same_head = (lane_head[:, None] == lane_head[None, :]).astype(np.float32)
    esel = np.zeros((2, LANE, 2 * hw), np.float32)
    for dd in range(2):
        for h in range(GDN_HEADS):
            esel[dd, M_A + dd * GDN_HEADS + h, h * GDN_DV:(h + 1) * GDN_DV] = 1.0
            esel[dd, M_B + dd * GDN_HEADS + h, hw + h * GDN_DV:hw + (h + 1) * GDN_DV] = 1.0
    wgk = jnp.zeros((2, LANE, GLA_HEADS * GLA_DK), F32)
    for dd in range(2):
        r0 = M_LR + dd * GLA_GATE_RANK
        wgk = wgk.at[dd, r0:r0 + GLA_GATE_RANK].set(gla_w_gk[i, dd])
    return {
        "w_in": w_in_p, "w_krT": w_krT, "w_uq": w_uq, "w_kT": w_kT, "w_v": w_v,
        "q_norm": mla_q_norm[i][None], "kv_norm": mla_kv_norm[i][None],
        "gdn_conv_w": gdn_conv_w[i], "gdn_esel": jnp.asarray(esel).astype(BF16),
        "gdn_alog": jnp.repeat(gdn_a_log[i], GDN_DV, axis=1)[:, None, :],
        "gdn_dtb": jnp.repeat(gdn_dt_bias[i], GDN_DV, axis=1)[:, None, :],
        "gsum": jnp.asarray(same_head).astype(BF16),
        "gla_wgk": wgk.astype(BF16), "gla_bgk": gla_b_gk[i][:, None, :],
        "gmat": jnp.asarray(same_head / GDN_DV).astype(BF16), "gdn_norm": jnp.tile(gdn_norm[i], GDN_HEADS)[None],
        "gla_norm": jnp.tile(gla_norm[i], GLA_HEADS)[None], "w_out": w_out[i].astype(BF16),
        "ffn_w_in": ffn_w_in[i].astype(BF16), "ffn_b_in": ffn_b_in[i][None], "ffn_conv_w": ffn_conv_w[i],
        "ffn_conv_b": ffn_conv_b[i][None], "ffn_w_out": ffn_w_out[i].astype(BF16),
    }


def kernel(x, c, ctx, c_ctx, w_ada, b_ada, norm_mix_pre, norm_mix_post, norm_ffn_pre, norm_ffn_post,
           w_in, mla_q_norm, mla_w_uq, mla_kv_norm, mla_w_ukv, gdn_conv_w, gdn_a_log, gdn_dt_bias,
           gdn_norm, gla_w_gk, gla_b_gk, gla_norm, w_out, ffn_w_in, ffn_b_in, ffn_conv_w, ffn_conv_b,
           ffn_w_out):
    bsz, t_latent, d = x.shape
    assert ctx.shape[1] == CTX_LEN and t_latent % TM == 0 and t_latent % GRID_W == 0
    depth = w_ada.shape[0]
    xs = jnp.concatenate([x, ctx], axis=1)
    mod_rows = -(-(bsz + 1) // SUBLANE) * SUBLANE
    cvec = jnp.concatenate([c, c_ctx[None], jnp.zeros((mod_rows - bsz - 1, d), F32)], axis=0)
    mods_all = ada_modulation(cvec, w_ada, b_ada).reshape(depth, mod_rows, N_MOD, d)
    tabs = _rope_tables(t_latent)
    for i in range(depth):
        mods = mods_all[i]
        wts = _prep_layer(i, w_in, mla_q_norm, mla_w_uq, mla_kv_norm, mla_w_ukv, gdn_conv_w, gdn_a_log,
                          gdn_dt_bias, gdn_norm, gla_w_gk, gla_b_gk, gla_norm, w_out, ffn_w_in, ffn_b_in,
                          ffn_conv_w, ffn_conv_b, ffn_w_out)
        q, kT, v, gqkv, gz, lq, lk, lv, lg, misc = pre_mix(xs, mods, norm_mix_pre[i][None], wts, tabs)
        mla_o = mla_attention(q, kT, v)
        gdn_f, gdn_r = gdn_mixer(gqkv, misc, wts)
        gla_f, gla_r = gla_mixer(lq, lk, lv, misc, wts)
        xs = post_mix(xs, mods, mla_o, gdn_f, gdn_r, gz, gla_f, gla_r, lg, wts, norm_mix_post[i][None])
        xs = conv_ffn(xs, mods, norm_ffn_pre[i][None], wts, norm_ffn_post[i][None])
    return xs[:, :t_latent]
```

```python
import functools
import math

import numpy as np
import jax
import jax.numpy as jnp
from jax import lax
from jax.experimental import pallas as pl
from jax.experimental.pallas import tpu as pltpu

F32 = jnp.float32
BF16 = jnp.bfloat16

D_MODEL = 1024
DEPTH = 4
GRID_W = 64
CTX_LEN = 256
N_MOD = 6
NORM_EPS = 1e-6

MLA_HEADS = 8
MLA_Q_RANK = 384
MLA_KV_RANK = 256
MLA_NOPE = 64
MLA_ROPE = 32
MLA_V = 64
ROPE_THETA = 10000.0

GDN_HEADS = 4
GDN_DK = 64
GDN_DV = 64
GDN_CHUNK = 64

GLA_HEADS = 4
GLA_DK = 32
GLA_DV = 64
GLA_GATE_RANK = 16
GLA_NORMALIZER = 16.0
GLA_CHUNK = 16

FFN_HIDDEN = 2560
D_MIX = MLA_HEADS * MLA_V + GDN_HEADS * GDN_DV + GLA_HEADS * GLA_DV
GDN_QKV = GDN_HEADS * (2 * GDN_DK + GDN_DV)
IN_SIZES = (MLA_Q_RANK, MLA_KV_RANK, MLA_ROPE,
            GDN_QKV, GDN_HEADS * GDN_DV, 2 * GDN_HEADS, 2 * GDN_HEADS,
            GLA_HEADS * GLA_DK, GLA_HEADS * GLA_DK, GLA_HEADS * GLA_DV, GLA_HEADS * GLA_DV,
            2 * GLA_GATE_RANK)
IN_OFFS = tuple(int(s) for s in np.cumsum((0,) + IN_SIZES))

LANE = 128
SUBLANE = 8
TM = CTX_LEN
TQ = 2 * TM
KV_GROUP = 4
HALO = SUBLANE
HEAD_PAD = LANE
FFN_CHUNK = 512
RC = 64
RNB = 2
assert RC == GDN_DK
VMEM_LIMIT = 56 * 1024 * 1024

P_CQ = 0
P_CKV = P_CQ + MLA_Q_RANK
P_GQKV = P_CKV + MLA_KV_RANK
P_GZ = P_GQKV + GDN_QKV
P_LQ = P_GZ + GDN_HEADS * GDN_DV
P_LK = P_LQ + GLA_HEADS * GLA_DK
P_LV = P_LK + GLA_HEADS * GLA_DK
P_LG = P_LV + GLA_HEADS * GLA_DV
P_MISC = P_LG + GLA_HEADS * GLA_DV
P_TOTAL = P_MISC + LANE
M_KR = 0
M_A = MLA_ROPE
M_B = M_A + 2 * GDN_HEADS
M_LR = M_B + 2 * GDN_HEADS


def _cparams(sem, vmem=VMEM_LIMIT):
    return pltpu.CompilerParams(dimension_semantics=sem, vmem_limit_bytes=vmem)


def _rms(x, g):
    return x * lax.rsqrt(jnp.mean(x * x, axis=-1, keepdims=True) + NORM_EPS) * g


def _silu(x):
    return x * jax.nn.sigmoid(x)


def _softplus(x):
    return jnp.maximum(x, 0.0) + jnp.log1p(jnp.exp(-jnp.abs(x)))


def _dot(a, b):
    return jnp.dot(a, b, preferred_element_type=F32)


def _dot_t(a, b):
    return lax.dot_general(a, b, (((1,), (1,)), ((), ())), preferred_element_type=F32)


def _ada_kernel(c_ref, w_ref, b_ref, o_ref):
    o_ref[0] = _dot(_silu(c_ref[...]), w_ref[0]) + b_ref[0]


def ada_modulation(cvec, w_ada, b_ada):
    depth, d, n = w_ada.shape
    rows = cvec.shape[0]
    tn = n // 4
    return pl.pallas_call(
        _ada_kernel,
        out_shape=jax.ShapeDtypeStruct((depth, rows, n), F32),
        grid=(depth, n // tn),
        in_specs=[pl.BlockSpec((rows, d), lambda l, j: (0, 0)),
                  pl.BlockSpec((1, d, tn), lambda l, j: (l, 0, j)),
                  pl.BlockSpec((1, 1, tn), lambda l, j: (l, 0, j))],
        out_specs=pl.BlockSpec((1, rows, tn), lambda l, j: (l, 0, j)),
        compiler_params=_cparams(("arbitrary", "arbitrary")),
        name="ada_modulation",
    )(cvec, w_ada, b_ada.reshape(depth, 1, n))


def _pre_mix_kernel(x_ref, mod_ref, g_ref, w_in_ref, qn_ref, kvn_ref, w_uq_ref, w_kT_ref, w_v_ref,
                    w_krT_ref, cq_ref, sq_ref, ckT_ref, skT_ref,
                    q_ref, kT_ref, v_ref, gqkv_ref, gz_ref, lq_ref, lk_ref, lv_ref, lg_ref, misc_ref):
    x = x_ref[0]
    shift = mod_ref[0, 0:1, :]
    scale = mod_ref[0, 1:2, :]
    h = _rms(x, g_ref[...]) * (1.0 + scale) + shift
    hb = h.astype(BF16)
    p = _dot(hb, w_in_ref[...])

    gqkv_ref[0] = p[:, P_GQKV:P_GZ]
    gz_ref[0] = p[:, P_GZ:P_LQ]
    lq_ref[0] = p[:, P_LQ:P_LK]
    lk_ref[0] = p[:, P_LK:P_LV]
    lv_ref[0] = p[:, P_LV:P_LG]
    lg_ref[0] = p[:, P_LG:P_MISC]
    misc_ref[0] = p[:, P_MISC:P_TOTAL]

    cqn = _rms(p[:, P_CQ:P_CKV], qn_ref[...]).astype(BF16)
    ckvn = _rms(p[:, P_CKV:P_GQKV], kvn_ref[...]).astype(BF16)
    nq = MLA_HEADS * HEAD_PAD
    qf = _dot(cqn, w_uq_ref[...])
    q = qf[:, :nq] * cq_ref[...] + qf[:, nq:] * sq_ref[...]
    vf = _dot(ckvn, w_v_ref[...])
    one_lane = lax.broadcasted_iota(jnp.int32, (1, HEAD_PAD), 1) == MLA_V
    kTn = _dot_t(w_kT_ref[...], ckvn)
    krT = _dot_t(w_krT_ref[...], hb)
    kr = (krT[:MLA_ROPE] * ckT_ref[...] + krT[MLA_ROPE:] * skT_ref[...]).astype(BF16)
    zpad = jnp.zeros((HEAD_PAD - MLA_NOPE - MLA_ROPE, kr.shape[1]), BF16)
    for hd in range(MLA_HEADS):
        q_ref[0, hd] = q[:, hd * HEAD_PAD:(hd + 1) * HEAD_PAD].astype(BF16)
        vh = vf[:, hd * HEAD_PAD:(hd + 1) * HEAD_PAD]
        v_ref[0, hd] = jnp.where(one_lane, 1.0, vh).astype(BF16)
        kT_ref[0, hd, 0, 0:MLA_NOPE, :] = kTn[hd * MLA_NOPE:(hd + 1) * MLA_NOPE].astype(BF16)
        kT_ref[0, hd, 0, MLA_NOPE:MLA_NOPE + MLA_ROPE, :] = kr
        kT_ref[0, hd, 0, MLA_NOPE + MLA_ROPE:, :] = zpad


def pre_mix(x, mods, norm_g, wts, tabs):
    bsz, seq, d = x.shape
    nt = seq // TM
    const = lambda *shape: pl.BlockSpec(shape, lambda i, b: (0,) * len(shape))
    tile = lambda w: pl.BlockSpec((1, TM, w), lambda i, b: (b, i, 0))
    nq = MLA_HEADS * HEAD_PAD
    in_specs = [
        tile(d),
        pl.BlockSpec((1, N_MOD, d), lambda i, b: (jnp.where(i == nt - 1, bsz, b), 0, 0)),
        const(1, d),
        const(d, P_TOTAL),
        const(1, MLA_Q_RANK),
        const(1, MLA_KV_RANK),
        const(MLA_Q_RANK, 2 * nq),
        const(MLA_HEADS * MLA_NOPE, MLA_KV_RANK),
        const(MLA_KV_RANK, nq),
        const(2 * MLA_ROPE, d),
        pl.BlockSpec((TM, nq), lambda i, b: (i, 0)),
        pl.BlockSpec((TM, nq), lambda i, b: (i, 0)),
        pl.BlockSpec((MLA_ROPE, TM), lambda i, b: (0, i)),
        pl.BlockSpec((MLA_ROPE, TM), lambda i, b: (0, i)),
    ]
    head_rows = pl.BlockSpec((1, MLA_HEADS, TM, HEAD_PAD), lambda i, b: (b, 0, i, 0))
    out_specs = [
        head_rows,
        pl.BlockSpec((1, MLA_HEADS, 1, HEAD_PAD, TM), lambda i, b: (b, 0, i, 0, 0)),
        head_rows,
        tile(GDN_QKV), tile(GDN_HEADS * GDN_DV),
        tile(GLA_HEADS * GLA_DK), tile(GLA_HEADS * GLA_DK),
        tile(GLA_HEADS * GLA_DV), tile(GLA_HEADS * GLA_DV),
        tile(LANE),
    ]
    sd = jax.ShapeDtypeStruct
    out_shape = [
        sd((bsz, MLA_HEADS, seq, HEAD_PAD), BF16),
        sd((bsz, MLA_HEADS, nt, HEAD_PAD, TM), BF16),
        sd((bsz, MLA_HEADS, seq, HEAD_PAD), BF16),
        sd((bsz, seq, GDN_QKV), F32), sd((bsz, seq, GDN_HEADS * GDN_DV), F32),
        sd((bsz, seq, GLA_HEADS * GLA_DK), F32), sd((bsz, seq, GLA_HEADS * GLA_DK), F32),
        sd((bsz, seq, GLA_HEADS * GLA_DV), F32), sd((bsz, seq, GLA_HEADS * GLA_DV), F32),
        sd((bsz, seq, LANE), F32),
    ]
    return pl.pallas_call(
        _pre_mix_kernel, out_shape=out_shape, grid=(nt, bsz), in_specs=in_specs, out_specs=out_specs,
        compiler_params=_cparams(("arbitrary", "arbitrary")), name="pre_mix",
    )(x, mods, norm_g, wts["w_in"], wts["q_norm"], wts["kv_norm"], wts["w_uq"], wts["w_kT"], wts["w_v"],
      wts["w_krT"], tabs["cq"], tabs["sq"], tabs["ckT"], tabs["skT"])


def _flash_kernel(q_ref, kT_ref, v_ref, *rest, n_groups, kv_group):
    o_ref = rest[-1]
    tq = q_ref.shape[2]
    ctx = kT_ref.shape[2] - 1
    qs = [q_ref[0, hh] for hh in range(2)]

    def attend(carry, kts, vs):
        ss = [_dot(qs[hh], kts[hh]) for hh in range(2)]
        out = []
        for hh in range(2):
            m, acc = carry[2 * hh], carry[2 * hh + 1]
            m_new = jnp.maximum(m, jnp.max(ss[hh], axis=-1, keepdims=True))
            p = jnp.exp2(ss[hh] - m_new).astype(BF16)
            alpha = jnp.exp2(m - m_new)
            out += [m_new, alpha * acc + _dot(p, vs[hh])]
        return tuple(out)

    init = (jnp.full((tq, 1), -1e30, F32), jnp.zeros((tq, HEAD_PAD), F32)) * 2
    carry = attend(init, [kT_ref[0, hh, ctx] for hh in range(2)],
                   [v_ref[0, hh, ctx * TM:(ctx + 1) * TM, :] for hh in range(2)])

    for j in range(n_groups):
        kts = [jnp.concatenate([kT_ref[0, hh, kv_group * j + c] for c in range(kv_group)], axis=1)
               for hh in range(2)]
        rows = slice(j * kv_group * TM, (j + 1) * kv_group * TM)
        carry = attend(carry, kts, [v_ref[0, hh, rows, :] for hh in range(2)])
    o_ref[0] = jnp.concatenate([carry[2 * hh + 1][:, :MLA_V] / carry[2 * hh + 1][:, MLA_V:MLA_V + 1]
                                for hh in range(2)], axis=-1)


def mla_attention(q, kT, v):
    bsz, nh, seq, _ = q.shape
    nt = seq // TM
    t_latent = seq - CTX_LEN
    kv_group = KV_GROUP if (nt - 1) % KV_GROUP == 0 else 2
    assert t_latent % TQ == 0 and (nt - 1) % kv_group == 0
    kv_specs = [pl.BlockSpec((1, 2, nt, HEAD_PAD, TM), lambda b, h, i: (b, h, 0, 0, 0)),
                pl.BlockSpec((1, 2, seq, HEAD_PAD), lambda b, h, i: (b, h, 0, 0))]
    sem = ("arbitrary", "arbitrary", "arbitrary")
    out_shape = jax.ShapeDtypeStruct((bsz, seq, nh * MLA_V), F32)
    o = pl.pallas_call(
        functools.partial(_flash_kernel, n_groups=(nt - 1) // kv_group, kv_group=kv_group),
        out_shape=out_shape,
        grid=(bsz, nh // 2, t_latent // TQ),
        in_specs=[pl.BlockSpec((1, 2, TQ, HEAD_PAD), lambda b, h, i: (b, h, i, 0))] + kv_specs,
        out_specs=pl.BlockSpec((1, TQ, 2 * MLA_V), lambda b, h, i: (b, i, h)),
        compiler_params=_cparams(sem), name="mla_attention_latent",
    )(q, kT, v)
    return pl.pallas_call(
        functools.partial(_flash_kernel, n_groups=0, kv_group=0),
        out_shape=out_shape,
        grid=(bsz, nh // 2, 1),
        in_specs=[pl.BlockSpec((1, 2, TM, HEAD_PAD), lambda b, h, i: (b, h, nt - 1, 0)),
                  pl.BlockSpec((1, 2, 1, HEAD_PAD, TM), lambda b, h, i: (b, h, nt - 1, 0, 0)),
                  pl.BlockSpec((1, 2, TM, HEAD_PAD), lambda b, h, i: (b, h, nt - 1, 0)),
                  pl.BlockSpec(memory_space=pl.ANY)],
        out_specs=pl.BlockSpec((1, TM, 2 * MLA_V), lambda b, h, i: (b, nt - 1, h)),
        input_output_aliases={3: 0},
        compiler_params=_cparams(sem), name="mla_attention_context",
    )(q, kT, v, o)


def _split3(x):
    h1 = x.astype(BF16)
    r1 = x - h1.astype(F32)
    h2 = r1.astype(BF16)
    h3 = (r1 - h2.astype(F32)).astype(BF16)
    return h1, h2, h3


def _sel_dot(sel, parts):
    return _dot(sel, parts[0]) + _dot(sel, parts[1]) + _dot(sel, parts[2])


def _dot_ta(a, b):
    return lax.dot_general(a, b, (((0,), (0,)), ((), ())), preferred_element_type=F32)


def _tri(n, d):
    r = lax.broadcasted_iota(jnp.int32, (n, n), 0)
    c = lax.broadcasted_iota(jnp.int32, (n, n), 1)
    return jnp.where(c <= r if d == 0 else c >= r, 1.0, 0.0).astype(BF16)


def _lane_head(width, per_head):
    return lax.broadcasted_iota(jnp.int32, (1, width), 1) // per_head


def _stack_heads(x, n_heads, per_head):
    lh = _lane_head(x.shape[1], per_head)
    return jnp.concatenate([jnp.where(lh == h, x, 0.0) for h in range(n_heads)], axis=0)


def _select_heads(per_head_results, per_head):
    n_heads = len(per_head_results)
    lh = _lane_head(per_head_results[0].shape[1], per_head) % n_heads
    out = jnp.where(lh == 0, per_head_results[0], 0.0)
    for h in range(1, n_heads):
        out = out + jnp.where(lh == h, per_head_results[h], 0.0)
    return out


def _chunk_maps(n_chunks, n_latent):
    fwd = lambda n: (n + n_latent) % n_chunks
    bwd = lambda n: n_chunks - 1 - n
    return fwd, bwd


def _job_specs(widths, n_chunks, n_latent, seq, nb):
    per = RC // HALO
    last = seq // HALO - 1
    specs = []
    for cmap in _chunk_maps(n_chunks, n_latent):
        chunk = lambda w, cmap=cmap: pl.BlockSpec((nb, RC, w), lambda b, n: (b, cmap(n), 0))
        if widths[0] is not None:
            w0 = widths[0]
            specs += [chunk(w0),
                      pl.BlockSpec((nb, HALO, w0), lambda b, n, cmap=cmap: (b, jnp.maximum(cmap(n) * per - 1, 0), 0)),
                      pl.BlockSpec((nb, HALO, w0), lambda b, n, cmap=cmap: (b, jnp.minimum((cmap(n) + 1) * per, last), 0))]
        specs += [chunk(w) for w in widths[1:]]
    return specs


def _lockstep(jobs):
    jobs = list(jobs)
    reqs = [next(job) for job in jobs]
    while reqs:
        results = [[fn(a, b) for fn, a, b in req] for req in reqs]
        nxt = []
        for job, res in zip(jobs, results):
            try:
                nxt.append(job.send(res))
            except StopIteration:
                pass
        assert not nxt or len(nxt) == len(jobs)
        reqs = nxt


def _gdn_job(d, c, n_chunks, n_latent, qkv, prev, nxt, misc, conv_w_ref, esel_ref,
             alog_ref, dtb_ref, gsum_ref, s_ref, s_idx, store_o):
    hw = GDN_HEADS * GDN_DK
    heads = range(GDN_HEADS)
    prev_ok = (c != 0) & (c != n_latent)
    next_ok = (c != n_latent - 1) & (c != n_chunks - 1)
    x = _silu(_conv3(jnp.concatenate([prev, qkv, nxt], axis=0), conv_w_ref, slice(None), prev_ok, next_ok))
    gsum = gsum_ref[...]
    q, k, v = x[:, :hw], x[:, hw:2 * hw], x[:, 2 * hw:]
    sq = [q * q, k * k]
    hi = [a.astype(BF16) for a in sq]
    lo = [(a - h.astype(F32)).astype(BF16) for a, h in zip(sq, hi)]
    r = yield [(_dot, a, gsum) for a in hi + lo]
    q = q * lax.rsqrt(r[0] + r[2] + NORM_EPS) * GDN_DK ** -0.5
    k = k * lax.rsqrt(r[1] + r[3] + NORM_EPS)

    r = yield [(_dot, p, esel_ref[d]) for p in _split3(misc)]
    ab = r[0] + r[1] + r[2]
    g = -jnp.exp(alog_ref[d]) * _softplus(ab[:, :hw] + dtb_ref[d])
    beta = jax.nn.sigmoid(ab[:, hw:])
    tri = _tri(RC, d)
    r = yield [(_dot, tri, p) for p in _split3(g)]
    gc = r[0] + r[1] + r[2]
    total = gc[RC - 1:RC] if d == 0 else gc[0:1]
    eg = jnp.exp(gc)
    kb = k * beta
    rhs = jnp.concatenate([v * beta, kb * eg], axis=1)
    qd = q * eg
    kend = k * jnp.exp(total - gc)

    k16 = k.astype(BF16)
    kk, qk = yield [(_dot_t, _stack_heads(kb, GDN_HEADS, GDN_DK).astype(BF16), k16),
                    (_dot_t, _stack_heads(q, GDN_HEADS, GDN_DK).astype(BF16), k16)]
    grow = _transpose(gc)
    gcol = jnp.concatenate([gc[:, h * GDN_DK:h * GDN_DK + RC] for h in range(GDN_HEADS)], axis=0)
    decay = jnp.exp(jnp.minimum(gcol - grow, 0.0))
    ii = lax.broadcasted_iota(jnp.int32, (GDN_HEADS * RC, RC), 0) % RC
    jj = lax.broadcasted_iota(jnp.int32, (GDN_HEADS * RC, RC), 1)
    strict = ii > jj if d == 0 else ii < jj
    lmat = jnp.where(strict, kk * decay, 0.0)
    amat = jnp.where(strict | (ii == jj), qk * decay, 0.0)

    eye = jnp.where(lax.broadcasted_iota(jnp.int32, (RC, RC), 0) == lax.broadcasted_iota(jnp.int32, (RC, RC), 1),
                    1.0, 0.0)
    n_sq = int(math.log2(RC)) - 1
    p = [lmat[h * RC:(h + 1) * RC] for h in heads]
    t = [eye - p[h] for h in heads]
    p = yield [(_dot, p[h], p[h]) for h in heads]
    for _ in range(n_sq - 1):
        r = yield [(_dot, jnp.concatenate([p[h], t[h]], axis=0), p[h]) for h in heads]
        p = [r[h][:RC] for h in heads]
        t = [t[h] + r[h][RC:] for h in heads]
    r = yield [(_dot, t[h], p[h]) for h in heads]
    t = [t[h] + r[h] for h in heads]
    r = yield [(_dot, t[h], rhs) for h in heads]
    uw = _select_heads(r, GDN_DK)
    u, w = uw[:, :hw], uw[:, hw:]

    s = s_ref[s_idx]
    (ws,) = yield [(_dot, jnp.concatenate([w, qd], axis=0), s)]
    v_new = u - ws[:RC]
    r = yield [(_dot, amat[h * RC:(h + 1) * RC], v_new) for h in heads] + [(_dot_ta, kend, v_new)]
    store_o(ws[RC:] + _select_heads(r[:GDN_HEADS], GDN_DV))
    rh = lax.broadcasted_iota(jnp.int32, (hw, hw), 0) // GDN_DK
    ch = lax.broadcasted_iota(jnp.int32, (hw, hw), 1) // GDN_DV
    s_ref[s_idx] = s * jnp.exp(total) + jnp.where(rh == ch, r[GDN_HEADS], 0.0)


def _sel_dot_r(parts, sel):
    return _dot(parts[0], sel) + _dot(parts[1], sel) + _dot(parts[2], sel)


def _transpose(x):
    n, m = x.shape
    xp = jnp.concatenate([x, jnp.zeros((LANE - n, m), x.dtype)], axis=0) if n < LANE else x
    return xp.T[:, :n]


def _store_at(ref, bi):
    def store(val):
        ref[bi] = val
    return store


def _gdn_kernel(*refs, n_chunks, n_latent):
    (qkv_f, prev_f, next_f, misc_f, qkv_r, prev_r, next_r, misc_r,
     conv_w, esel, alog, dtb, gsum, o_f, o_r, s_ref) = refs
    n = pl.program_id(1)

    @pl.when(n == 0)
    def _():
        s_ref[...] = jnp.zeros_like(s_ref)

    fwd, bwd = _chunk_maps(n_chunks, n_latent)
    consts = (conv_w, esel, alog, dtb, gsum, s_ref)
    jobs = []
    for bi in range(qkv_f.shape[0]):
        jobs.append(_gdn_job(0, fwd(n), n_chunks, n_latent, qkv_f[bi], prev_f[bi], next_f[bi], misc_f[bi],
                             *consts, 2 * bi, _store_at(o_f, bi)))
        jobs.append(_gdn_job(1, bwd(n), n_chunks, n_latent, qkv_r[bi], prev_r[bi], next_r[bi], misc_r[bi],
                             *consts, 2 * bi + 1, _store_at(o_r, bi)))
    _lockstep(jobs)


def gdn_mixer(gqkv, misc, wts):
    bsz, seq, _ = gqkv.shape
    n_chunks = seq // RC
    n_latent = (seq - CTX_LEN) // RC
    hw = GDN_HEADS * GDN_DV
    nb = RNB if bsz % RNB == 0 else 1
    const = lambda *shape: pl.BlockSpec(shape, lambda b, n: (0,) * len(shape))
    fwd, bwd = _chunk_maps(n_chunks, n_latent)
    out = jax.ShapeDtypeStruct((bsz, seq, hw), F32)
    return pl.pallas_call(
        functools.partial(_gdn_kernel, n_chunks=n_chunks, n_latent=n_latent),
        out_shape=[out, out],
        grid=(bsz // nb, n_chunks),
        in_specs=_job_specs((GDN_QKV, LANE), n_chunks, n_latent, seq, nb)
        + [const(3, GDN_QKV), const(2, LANE, 2 * hw), const(2, 1, hw), const(2, 1, hw), const(hw, hw)],
        out_specs=[pl.BlockSpec((nb, RC, hw), lambda b, n: (b, fwd(n), 0)),
                   pl.BlockSpec((nb, RC, hw), lambda b, n: (b, bwd(n), 0))],
        scratch_shapes=[pltpu.VMEM((2 * nb, hw, hw), F32)],
        compiler_params=_cparams(("arbitrary", "arbitrary")),
        name="gdn_mixer",
    )(gqkv, gqkv, gqkv, misc, gqkv, gqkv, gqkv, misc,
      wts["gdn_conv_w"], wts["gdn_esel"], wts["gdn_alog"], wts["gdn_dtb"], wts["gsum"])


def _gla_job(d, q, k, v, misc, wgk_ref, bgk_ref, st_ref, s_idx, store_o):
    heads = range(GLA_HEADS)
    q = q * GLA_DK ** -0.5
    (gk,) = yield [(_dot, misc.astype(BF16), wgk_ref[d])]
    la = -_softplus(-(gk + bgk_ref[d])) * (1.0 / GLA_NORMALIZER)
    tri = _tri(RC, d)
    r = yield [(_dot, tri, p) for p in _split3(la)]
    parts = _split3(r[0] + r[1] + r[2])
    b = parts[0].astype(F32) + parts[1].astype(F32) + parts[2].astype(F32)
    blast = b[RC - 1:RC] if d == 0 else b[0:1]

    ii = lax.broadcasted_iota(jnp.int32, (GLA_HEADS * RC, RC), 0) % RC
    jj = lax.broadcasted_iota(jnp.int32, (GLA_HEADS * RC, RC), 1)
    tt = lax.broadcasted_iota(jnp.int32, (RC, RC), 0)
    tc = lax.broadcasted_iota(jnp.int32, (RC, RC), 1)
    n_levels = int(math.log2(RC))
    sels = []
    for ls in range(n_levels):
        start = (tt >> (ls + 1)) << (ls + 1)
        mid = start + ((1 << ls) - 1 if d == 0 else (1 << ls))
        sels.append(jnp.where(tc == mid, 1.0, 0.0).astype(BF16))
    r = yield [(_dot, sel, p) for sel in sels for p in parts]
    lhs, rhs, pairs = [_stack_heads(q, GLA_HEADS, GLA_DK).astype(BF16)], [k.astype(BF16)], [ii == jj]
    for ls in range(n_levels):
        s = 1 << ls
        ref = r[3 * ls] + r[3 * ls + 1] + r[3 * ls + 2]
        lhs.append(_stack_heads(q * jnp.exp(jnp.minimum(b - ref, 0.0)), GLA_HEADS, GLA_DK).astype(BF16))
        rhs.append((k * jnp.exp(jnp.minimum(ref - b, 0.0))).astype(BF16))
        same = (ii >> (ls + 1)) == (jj >> (ls + 1))
        i_hi = (ii & (2 * s - 1)) >= s
        j_hi = (jj & (2 * s - 1)) >= s
        pairs.append(same & (i_hi & ~j_hi if d == 0 else ~i_hi & j_hi))
    r = yield [(_dot_t, a, bb) for a, bb in zip(lhs, rhs)]
    amat = jnp.where(pairs[0], r[0], 0.0)
    for pair, prod in zip(pairs[1:], r[1:]):
        amat = amat + jnp.where(pair, prod, 0.0)

    st = st_ref[s_idx]
    kend = k * jnp.exp(blast - b)
    r = yield ([(_dot, amat[h * RC:(h + 1) * RC], v) for h in heads]
               + [(_dot_t, q * jnp.exp(b), st), (_dot_ta, v, kend)])
    store_o(_select_heads(r[:GLA_HEADS], GLA_DV) + r[GLA_HEADS])
    rh = lax.broadcasted_iota(jnp.int32, st.shape, 0) // GLA_DV
    ch = lax.broadcasted_iota(jnp.int32, st.shape, 1) // GLA_DK
    st_ref[s_idx] = st * jnp.exp(blast) + jnp.where(rh == ch, r[GLA_HEADS + 1], 0.0)


def _gla_kernel(q_f, k_f, v_f, misc_f, q_r, k_r, v_r, misc_r, wgk, bgk, o_f, o_r, st_ref):
    @pl.when(pl.program_id(1) == 0)
    def _():
        st_ref[...] = jnp.zeros_like(st_ref)

    jobs = []
    for bi in range(q_f.shape[0]):
        jobs.append(_gla_job(0, q_f[bi], k_f[bi], v_f[bi], misc_f[bi], wgk, bgk, st_ref, 2 * bi, _store_at(o_f, bi)))
        jobs.append(_gla_job(1, q_r[bi], k_r[bi], v_r[bi], misc_r[bi], wgk, bgk, st_ref, 2 * bi + 1,
                             _store_at(o_r, bi)))
    _lockstep(jobs)


def gla_mixer(lq, lk, lv, misc, wts):
    bsz, seq, hk = lq.shape
    hv = lv.shape[2]
    n_chunks = seq // RC
    n_latent = (seq - CTX_LEN) // RC
    nb = RNB if bsz % RNB == 0 else 1
    const = lambda *shape: pl.BlockSpec(shape, lambda b, n: (0,) * len(shape))
    fwd, bwd = _chunk_maps(n_chunks, n_latent)
    out = jax.ShapeDtypeStruct((bsz, seq, hv), F32)
    return pl.pallas_call(
        _gla_kernel,
        out_shape=[out, out],
        grid=(bsz // nb, n_chunks),
        in_specs=_job_specs((None, hk, hk, hv, LANE), n_chunks, n_latent, seq, nb)
        + [const(2, LANE, hk), const(2, 1, hk)],
        out_specs=[pl.BlockSpec((nb, RC, hv), lambda b, n: (b, fwd(n), 0)),
                   pl.BlockSpec((nb, RC, hv), lambda b, n: (b, bwd(n), 0))],
        scratch_shapes=[pltpu.VMEM((2 * nb, hv, hk), F32)],
        compiler_params=_cparams(("arbitrary", "arbitrary")),
        name="gla_mixer",
    )(lq, lk, lv, misc, lq, lk, lv, misc, wts["gla_wgk"], wts["gla_bgk"])


def _group_mean_sq(x, gmat):
    xsq = x * x
    hi = xsq.astype(BF16)
    lo = (xsq - hi.astype(F32)).astype(BF16)
    return _dot(hi, gmat) + _dot(lo, gmat)


def _post_mix_kernel(x_ref, mod_ref, mla_ref, gdn_f_ref, gdn_r_ref, gz_ref, gla_f_ref, gla_r_ref, lg_ref,
                     gmat_ref, gdn_g_ref, gla_g_ref, w_out_ref, post_g_ref, o_ref):
    gmat = gmat_ref[...]
    gdn = gdn_f_ref[0] + gdn_r_ref[0]
    gdn = gdn * lax.rsqrt(_group_mean_sq(gdn, gmat) + NORM_EPS) * gdn_g_ref[...] * _silu(gz_ref[0])
    gla = gla_f_ref[0] + gla_r_ref[0]
    gla = gla * lax.rsqrt(_group_mean_sq(gla, gmat) + NORM_EPS) * gla_g_ref[...] * _silu(lg_ref[0])
    merged = jnp.concatenate([mla_ref[0], gdn, gla], axis=-1).astype(BF16)
    y = _dot(merged, w_out_ref[...])
    gate = mod_ref[0, 2:3, :]
    o_ref[0] = x_ref[0] + gate * _rms(y, post_g_ref[...])


def post_mix(x, mods, mla_o, gdn_f, gdn_r, gz, gla_f, gla_r, lg, wts, post_g):
    bsz, seq, d = x.shape
    nt = seq // TM
    const = lambda *shape: pl.BlockSpec(shape, lambda b, i: (0,) * len(shape))
    tile = lambda w: pl.BlockSpec((1, TM, w), lambda b, i: (b, i, 0))
    hw = GDN_HEADS * GDN_DV
    return pl.pallas_call(
        _post_mix_kernel,
        out_shape=jax.ShapeDtypeStruct(x.shape, F32),
        grid=(bsz, nt),
        in_specs=[tile(d),
                  pl.BlockSpec((1, N_MOD, d), lambda b, i: (jnp.where(i == nt - 1, bsz, b), 0, 0)),
                  tile(MLA_HEADS * MLA_V), tile(hw), tile(hw), tile(hw), tile(hw), tile(hw), tile(hw),
                  const(hw, hw), const(1, hw), const(1, hw), const(D_MIX, d), const(1, d)],
        out_specs=tile(d),
        compiler_params=_cparams(("arbitrary", "arbitrary")),
        name="post_mix",
    )(x, mods, mla_o, gdn_f, gdn_r, gz, gla_f, gla_r, lg, wts["gmat"], wts["gdn_norm"], wts["gla_norm"],
      wts["w_out"], post_g)


def _conv3(z, w_ref, cols, prev_ok, next_ok):
    rows = z.shape[0]
    n = rows - 2 * HALO
    r = lax.broadcasted_iota(jnp.int32, (rows, 1), 0)
    z = jnp.where((r == HALO - 1) & jnp.logical_not(prev_ok), 0.0, z)
    z = jnp.where((r == HALO + n) & jnp.logical_not(next_ok), 0.0, z)
    zm = pltpu.roll(z, 1, axis=0)[HALO:HALO + n]
    zp = pltpu.roll(z, rows - 1, axis=0)[HALO:HALO + n]
    zc = z[HALO:HALO + n]
    return zm * w_ref[0:1, cols] + zc * w_ref[1:2, cols] + zp * w_ref[2:3, cols]


def _ffn_kernel(x_ref, xp_ref, xn_ref, mod_ref, pre_g_ref, w_in_ref, b_in_ref, cw_ref, cb_ref,
                w_out_ref, post_g_ref, o_ref):
    i = pl.program_id(1)
    nt = pl.num_programs(1)
    prev_ok = (i > 0) & (i < nt - 1)
    next_ok = i < nt - 2
    x = x_ref[0]
    xx = jnp.concatenate([xp_ref[0], x, xn_ref[0]], axis=0)
    shift = mod_ref[0, 3:4, :]
    scale = mod_ref[0, 4:5, :]
    hb = (_rms(xx, pre_g_ref[...]) * (1.0 + scale) + shift).astype(BF16)
    n_chunks = FFN_HIDDEN // FFN_CHUNK
    cols = [(slice(j * FFN_CHUNK, (j + 1) * FFN_CHUNK),
             slice(FFN_HIDDEN + j * FFN_CHUNK, FFN_HIDDEN + (j + 1) * FFN_CHUNK)) for j in range(n_chunks)]
    in_proj = lambda j: [_dot(hb, w_in_ref[:, c]) for c in cols[j]]
    acc = jnp.zeros((TM, x.shape[1]), F32)
    z_next = in_proj(0)
    for j in range(n_chunks):
        (za, zg), (ca, cg) = z_next, cols[j]
        if j + 1 < n_chunks:
            z_next = in_proj(j + 1)
        a = _conv3(za + b_in_ref[:, ca], cw_ref, ca, prev_ok, next_ok) + cb_ref[:, ca]
        g = _conv3(zg + b_in_ref[:, cg], cw_ref, cg, prev_ok, next_ok) + cb_ref[:, cg]
        act = (a * _silu(g)).astype(BF16)
        acc = acc + _dot(act, w_out_ref[ca, :])
    gate = mod_ref[0, 5:6, :]
    o_ref[0] = x + gate * _rms(acc, post_g_ref[...])


def conv_ffn(x, mods, pre_g, wts, post_g):
    bsz, seq, d = x.shape
    nt = seq // TM
    nb = seq // HALO
    per = TM // HALO
    const = lambda *shape: pl.BlockSpec(shape, lambda b, i: (0,) * len(shape))
    tile = pl.BlockSpec((1, TM, d), lambda b, i: (b, i, 0))
    return pl.pallas_call(
        _ffn_kernel,
        out_shape=jax.ShapeDtypeStruct(x.shape, F32),
        grid=(bsz, nt),
        in_specs=[tile,
                  pl.BlockSpec((1, HALO, d), lambda b, i: (b, jnp.maximum(i * per - 1, 0), 0)),
                  pl.BlockSpec((1, HALO, d), lambda b, i: (b, jnp.minimum((i + 1) * per, nb - 1), 0)),
                  pl.BlockSpec((1, N_MOD, d), lambda b, i: (jnp.where(i == nt - 1, bsz, b), 0, 0)),
                  const(1, d), const(d, 2 * FFN_HIDDEN), const(1, 2 * FFN_HIDDEN),
                  const(3, 2 * FFN_HIDDEN), const(1, 2 * FFN_HIDDEN), const(FFN_HIDDEN, d), const(1, d)],
        out_specs=tile,
        compiler_params=_cparams(("arbitrary", "arbitrary")),
        name="conv_ffn",
    )(x, x, x, mods, pre_g, wts["ffn_w_in"], wts["ffn_b_in"], wts["ffn_conv_w"], wts["ffn_conv_b"],
      wts["ffn_w_out"], post_g)


def _rope_tables(t_latent):
    n = MLA_ROPE // 4
    t = jnp.arange(t_latent)
    row = (t // GRID_W).astype(F32)
    col = (t % GRID_W).astype(F32)
    inv = ROPE_THETA ** (-jnp.arange(n, dtype=F32) / n)
    ang = jnp.stack([row[:, None] * inv, col[:, None] * inv], axis=1)
    cos, sin = jnp.cos(ang), jnp.sin(ang)
    c32 = jnp.stack([cos, cos], axis=2).reshape(t_latent, MLA_ROPE)
    s32 = jnp.stack([-sin, sin], axis=2).reshape(t_latent, MLA_ROPE)
    c32 = jnp.concatenate([c32, jnp.ones((CTX_LEN, MLA_ROPE), F32)], axis=0)
    s32 = jnp.concatenate([s32, jnp.zeros((CTX_LEN, MLA_ROPE), F32)], axis=0)
    seq = CTX_LEN + t_latent
    qscale = (MLA_NOPE + MLA_ROPE) ** -0.5 * math.log2(math.e)
    pad = HEAD_PAD - MLA_NOPE - MLA_ROPE
    cq = jnp.concatenate([jnp.ones((seq, MLA_NOPE), F32), c32, jnp.zeros((seq, pad), F32)], axis=1) * qscale
    sq = jnp.concatenate([jnp.zeros((seq, MLA_NOPE), F32), s32, jnp.zeros((seq, pad), F32)], axis=1) * qscale
    return {"cq": jnp.tile(cq, (1, MLA_HEADS)), "sq": jnp.tile(sq, (1, MLA_HEADS)),
            "ckT": c32.T, "skT": s32.T}


_ROPE_PARTNER = np.arange(MLA_ROPE) ^ (MLA_ROPE // 4)


def _prep_layer(i, w_in, mla_q_norm, mla_w_uq, mla_kv_norm, mla_w_ukv, gdn_conv_w, gdn_a_log, gdn_dt_bias,
                gdn_norm, gla_w_gk, gla_b_gk, gla_norm, w_out, ffn_w_in, ffn_b_in, ffn_conv_w, ffn_conv_b,
                ffn_w_out):
    o = IN_OFFS
    w = w_in[i]
    d = w.shape[0]
    piece = lambda k: w[:, o[k]:o[k + 1]]
    misc_pad = LANE - (MLA_ROPE + 4 * GDN_HEADS + 2 * GLA_GATE_RANK)
    w_in_p = jnp.concatenate(
        [piece(0), piece(1), piece(3), piece(4), piece(7), piece(8), piece(9), piece(10),
         piece(2), piece(5), piece(6), piece(11), jnp.zeros((d, misc_pad), F32)], axis=1).astype(BF16)
    kr = piece(2)
    w_krT = jnp.concatenate([kr, kr[:, _ROPE_PARTNER]], axis=1).T.astype(BF16)

    hq = MLA_NOPE + MLA_ROPE
    uq = mla_w_uq[i].reshape(MLA_Q_RANK, MLA_HEADS, hq)
    zq = lambda n: jnp.zeros((MLA_Q_RANK, MLA_HEADS, n), F32)
    plain = jnp.concatenate([uq, zq(HEAD_PAD - hq)], axis=2)
    partner = jnp.concatenate([zq(MLA_NOPE), uq[:, :, MLA_NOPE:][:, :, _ROPE_PARTNER], zq(HEAD_PAD - hq)], axis=2)
    w_uq = jnp.concatenate([plain.reshape(MLA_Q_RANK, -1), partner.reshape(MLA_Q_RANK, -1)], axis=1).astype(BF16)

    ukv = mla_w_ukv[i].reshape(MLA_KV_RANK, MLA_HEADS, MLA_NOPE + MLA_V)
    w_kT = ukv[:, :, :MLA_NOPE].reshape(MLA_KV_RANK, -1).T.astype(BF16)
    w_v = jnp.concatenate([ukv[:, :, MLA_NOPE:], jnp.zeros((MLA_KV_RANK, MLA_HEADS, HEAD_PAD - MLA_V), F32)],
                          axis=2).reshape(MLA_KV_RANK, -1).astype(BF16)

    hw = GDN_HEADS * GDN_DV
    lane_head = np.arange(hw) // GDN_DV
    same_head = (lane_head[:, None] == lane_head[None, :]).astype(np.float32)
    esel = np.zeros((2, LANE, 2 * hw), np.float32)
    for dd in range(2):
        for h in range(GDN_HEADS):
            esel[dd, M_A + dd * GDN_HEADS + h, h * GDN_DV:(h + 1) * GDN_DV] = 1.0
            esel[dd, M_B + dd * GDN_HEADS + h, hw + h * GDN_DV:hw + (h + 1) * GDN_DV] = 1.0
    wgk = jnp.zeros((2, LANE, GLA_HEADS * GLA_DK), F32)
    for dd in range(2):
        r0 = M_LR + dd * GLA_GATE_RANK
        wgk = wgk.at[dd, r0:r0 + GLA_GATE_RANK].set(gla_w_gk[i, dd])
    return {
        "w_in": w_in_p, "w_krT": w_krT, "w_uq": w_uq, "w_kT": w_kT, "w_v": w_v,
        "q_norm": mla_q_norm[i][None], "kv_norm": mla_kv_norm[i][None],
        "gdn_conv_w": gdn_conv_w[i], "gdn_esel": jnp.asarray(esel).astype(BF16),
        "gdn_alog": jnp.repeat(gdn_a_log[i], GDN_DV, axis=1)[:, None, :],
        "gdn_dtb": jnp.repeat(gdn_dt_bias[i], GDN_DV, axis=1)[:, None, :],
        "gsum": jnp.asarray(same_head).astype(BF16),
        "gla_wgk": wgk.astype(BF16), "gla_bgk": gla_b_gk[i][:, None, :],
        "gmat": jnp.asarray(same_head / GDN_DV).astype(BF16), "gdn_norm": jnp.tile(gdn_norm[i], GDN_HEADS)[None],
        "gla_norm": jnp.tile(gla_norm[i], GLA_HEADS)[None], "w_out": w_out[i].astype(BF16),
        "ffn_w_in": ffn_w_in[i].astype(BF16), "ffn_b_in": ffn_b_in[i][None], "ffn_conv_w": ffn_conv_w[i],
        "ffn_conv_b": ffn_conv_b[i][None], "ffn_w_out": ffn_w_out[i].astype(BF16),
    }


def kernel(x, c, ctx, c_ctx, w_ada, b_ada, norm_mix_pre, norm_mix_post, norm_ffn_pre, norm_ffn_post,
           w_in, mla_q_norm, mla_w_uq, mla_kv_norm, mla_w_ukv, gdn_conv_w, gdn_a_log, gdn_dt_bias,
           gdn_norm, gla_w_gk, gla_b_gk, gla_norm, w_out, ffn_w_in, ffn_b_in, ffn_conv_w, ffn_conv_b,
           ffn_w_out):
    bsz, t_latent, d = x.shape
    assert ctx.shape[1] == CTX_LEN and t_latent % TM == 0 and t_latent % GRID_W == 0
    depth = w_ada.shape[0]
    xs = jnp.concatenate([x, ctx], axis=1)
    mod_rows = -(-(bsz + 1) // SUBLANE) * SUBLANE
    cvec = jnp.concatenate([c, c_ctx[None], jnp.zeros((mod_rows - bsz - 1, d), F32)], axis=0)
    mods_all = ada_modulation(cvec, w_ada, b_ada).reshape(depth, mod_rows, N_MOD, d)
    tabs = _rope_tables(t_latent)
    for i in range(depth):
        mods = mods_all[i]
        wts = _prep_layer(i, w_in, mla_q_norm, mla_w_uq, mla_kv_norm, mla_w_ukv, gdn_conv_w, gdn_a_log,
                          gdn_dt_bias, gdn_norm, gla_w_gk, gla_b_gk, gla_norm, w_out, ffn_w_in, ffn_b_in,
                          ffn_conv_w, ffn_conv_b, ffn_w_out)
        q, kT, v, gqkv, gz, lq, lk, lv, lg, misc = pre_mix(xs, mods, norm_mix_pre[i][None], wts, tabs)
        mla_o = mla_attention(q, kT, v)
        gdn_f, gdn_r = gdn_mixer(gqkv, misc, wts)
        gla_f, gla_r = gla_mixer(lq, lk, lv, misc, wts)
        xs = post_mix(xs, mods, mla_o, gdn_f, gdn_r, gz, gla_f, gla_r, lg, wts, norm_mix_post[i][None])
        xs = conv_ffn(xs, mods, norm_ffn_pre[i][None], wts, norm_ffn_post[i][None])
    return xs[:, :t_latent]
```

```python
import functools
import math

import numpy as np
import jax
import jax.numpy as jnp
from jax import lax
from jax.experimental import pallas as pl
from jax.experimental.pallas import tpu as pltpu

F32 = jnp.float32
BF16 = jnp.bfloat16

D_MODEL = 1024
DEPTH = 4
GRID_W = 64
CTX_LEN = 256
N_MOD = 6
NORM_EPS = 1e-6

MLA_HEADS = 8
MLA_Q_RANK = 384
MLA_KV_RANK = 256
MLA_NOPE = 64
MLA_ROPE = 32
MLA_V = 64
ROPE_THETA = 10000.0

GDN_HEADS = 4
GDN_DK = 64
GDN_DV = 64
GDN_CHUNK = 64

GLA_HEADS = 4
GLA_DK = 32
GLA_DV = 64
GLA_GATE_RANK = 16
GLA_NORMALIZER = 16.0
GLA_CHUNK = 16

FFN_HIDDEN = 2560
D_MIX = MLA_HEADS * MLA_V + GDN_HEADS * GDN_DV + GLA_HEADS * GLA_DV
GDN_QKV = GDN_HEADS * (2 * GDN_DK + GDN_DV)
IN_SIZES = (MLA_Q_RANK, MLA_KV_RANK, MLA_ROPE,
            GDN_QKV, GDN_HEADS * GDN_DV, 2 * GDN_HEADS, 2 * GDN_HEADS,
            GLA_HEADS * GLA_DK, GLA_HEADS * GLA_DK, GLA_HEADS * GLA_DV, GLA_HEADS * GLA_DV,
            2 * GLA_GATE_RANK)
IN_OFFS = tuple(int(s) for s in np.cumsum((0,) + IN_SIZES))

LANE = 128
SUBLANE = 8
TM = CTX_LEN
TQ = 2 * TM
KV_GROUP = 4
HALO = SUBLANE
HEAD_PAD = LANE
FFN_CHUNK = 512
RC = 64
RNB = 4
assert RC == GDN_DK
VMEM_LIMIT = 56 * 1024 * 1024

P_CQ = 0
P_CKV = P_CQ + MLA_Q_RANK
P_GQKV = P_CKV + MLA_KV_RANK
P_GZ = P_GQKV + GDN_QKV
P_LQ = P_GZ + GDN_HEADS * GDN_DV
P_LK = P_LQ + GLA_HEADS * GLA_DK
P_LV = P_LK + GLA_HEADS * GLA_DK
P_LG = P_LV + GLA_HEADS * GLA_DV
P_MISC = P_LG + GLA_HEADS * GLA_DV
P_TOTAL = P_MISC + LANE
M_KR = 0
M_A = MLA_ROPE
M_B = M_A + 2 * GDN_HEADS
M_LR = M_B + 2 * GDN_HEADS


def _cparams(sem, vmem=VMEM_LIMIT):
    return pltpu.CompilerParams(dimension_semantics=sem, vmem_limit_bytes=vmem)


def _rms(x, g):
    return x * lax.rsqrt(jnp.mean(x * x, axis=-1, keepdims=True) + NORM_EPS) * g


def _silu(x):
    return x * jax.nn.sigmoid(x)


def _softplus(x):
    return jnp.maximum(x, 0.0) + jnp.log1p(jnp.exp(-jnp.abs(x)))


def _dot(a, b):
    return jnp.dot(a, b, preferred_element_type=F32)


def _dot_t(a, b):
    return lax.dot_general(a, b, (((1,), (1,)), ((), ())), preferred_element_type=F32)


def _ada_kernel(c_ref, w_ref, b_ref, o_ref):
    o_ref[0] = _dot(_silu(c_ref[...]), w_ref[0]) + b_ref[0]


def ada_modulation(cvec, w_ada, b_ada):
    depth, d, n = w_ada.shape
    rows = cvec.shape[0]
    tn = n // 4
    return pl.pallas_call(
        _ada_kernel,
        out_shape=jax.ShapeDtypeStruct((depth, rows, n), F32),
        grid=(depth, n // tn),
        in_specs=[pl.BlockSpec((rows, d), lambda l, j: (0, 0)),
                  pl.BlockSpec((1, d, tn), lambda l, j: (l, 0, j)),
                  pl.BlockSpec((1, 1, tn), lambda l, j: (l, 0, j))],
        out_specs=pl.BlockSpec((1, rows, tn), lambda l, j: (l, 0, j)),
        compiler_params=_cparams(("arbitrary", "arbitrary")),
        name="ada_modulation",
    )(cvec, w_ada, b_ada.reshape(depth, 1, n))


def _pre_mix_kernel(x_ref, mod_ref, g_ref, w_in_ref, qn_ref, kvn_ref, w_uq_ref, w_kT_ref, w_v_ref,
                    w_krT_ref, cq_ref, sq_ref, ckT_ref, skT_ref,
                    q_ref, kT_ref, v_ref, gqkv_ref, gz_ref, lq_ref, lk_ref, lv_ref, lg_ref, misc_ref):
    x = x_ref[0]
    shift = mod_ref[0, 0:1, :]
    scale = mod_ref[0, 1:2, :]
    h = _rms(x, g_ref[...]) * (1.0 + scale) + shift
    hb = h.astype(BF16)
    p = _dot(hb, w_in_ref[...])

    gqkv_ref[0] = p[:, P_GQKV:P_GZ]
    gz_ref[0] = p[:, P_GZ:P_LQ]
    lq_ref[0] = p[:, P_LQ:P_LK]
    lk_ref[0] = p[:, P_LK:P_LV]
    lv_ref[0] = p[:, P_LV:P_LG]
    lg_ref[0] = p[:, P_LG:P_MISC]
    misc_ref[0] = p[:, P_MISC:P_TOTAL]

    cqn = _rms(p[:, P_CQ:P_CKV], qn_ref[...]).astype(BF16)
    ckvn = _rms(p[:, P_CKV:P_GQKV], kvn_ref[...]).astype(BF16)
    nq = MLA_HEADS * HEAD_PAD
    qf = _dot(cqn, w_uq_ref[...])
    q = qf[:, :nq] * cq_ref[...] + qf[:, nq:] * sq_ref[...]
    vf = _dot(ckvn, w_v_ref[...])
    one_lane = lax.broadcasted_iota(jnp.int32, (1, HEAD_PAD), 1) == MLA_V
    kTn = _dot_t(w_kT_ref[...], ckvn)
    krT = _dot_t(w_krT_ref[...], hb)
    kr = (krT[:MLA_ROPE] * ckT_ref[...] + krT[MLA_ROPE:] * skT_ref[...]).astype(BF16)
    zpad = jnp.zeros((HEAD_PAD - MLA_NOPE - MLA_ROPE, kr.shape[1]), BF16)
    for hd in range(MLA_HEADS):
        q_ref[0, hd] = q[:, hd * HEAD_PAD:(hd + 1) * HEAD_PAD].astype(BF16)
        vh = vf[:, hd * HEAD_PAD:(hd + 1) * HEAD_PAD]
        v_ref[0, hd] = jnp.where(one_lane, 1.0, vh).astype(BF16)
        kT_ref[0, hd, 0, 0:MLA_NOPE, :] = kTn[hd * MLA_NOPE:(hd + 1) * MLA_NOPE].astype(BF16)
        kT_ref[0, hd, 0, MLA_NOPE:MLA_NOPE + MLA_ROPE, :] = kr
        kT_ref[0, hd, 0, MLA_NOPE + MLA_ROPE:, :] = zpad


def pre_mix(x, mods, norm_g, wts, tabs):
    bsz, seq, d = x.shape
    nt = seq // TM
    const = lambda *shape: pl.BlockSpec(shape, lambda i, b: (0,) * len(shape))
    tile = lambda w: pl.BlockSpec((1, TM, w), lambda i, b: (b, i, 0))
    nq = MLA_HEADS * HEAD_PAD
    in_specs = [
        tile(d),
        pl.BlockSpec((1, N_MOD, d), lambda i, b: (jnp.where(i == nt - 1, bsz, b), 0, 0)),
        const(1, d),
        const(d, P_TOTAL),
        const(1, MLA_Q_RANK),
        const(1, MLA_KV_RANK),
        const(MLA_Q_RANK, 2 * nq),
        const(MLA_HEADS * MLA_NOPE, MLA_KV_RANK),
        const(MLA_KV_RANK, nq),
        const(2 * MLA_ROPE, d),
        pl.BlockSpec((TM, nq), lambda i, b: (i, 0)),
        pl.BlockSpec((TM, nq), lambda i, b: (i, 0)),
        pl.BlockSpec((MLA_ROPE, TM), lambda i, b: (0, i)),
        pl.BlockSpec((MLA_ROPE, TM), lambda i, b: (0, i)),
    ]
    head_rows = pl.BlockSpec((1, MLA_HEADS, TM, HEAD_PAD), lambda i, b: (b, 0, i, 0))
    out_specs = [
        head_rows,
        pl.BlockSpec((1, MLA_HEADS, 1, HEAD_PAD, TM), lambda i, b: (b, 0, i, 0, 0)),
        head_rows,
        tile(GDN_QKV), tile(GDN_HEADS * GDN_DV),
        tile(GLA_HEADS * GLA_DK), tile(GLA_HEADS * GLA_DK),
        tile(GLA_HEADS * GLA_DV), tile(GLA_HEADS * GLA_DV),
        tile(LANE),
    ]
    sd = jax.ShapeDtypeStruct
    out_shape = [
        sd((bsz, MLA_HEADS, seq, HEAD_PAD), BF16),
        sd((bsz, MLA_HEADS, nt, HEAD_PAD, TM), BF16),
        sd((bsz, MLA_HEADS, seq, HEAD_PAD), BF16),
        sd((bsz, seq, GDN_QKV), F32), sd((bsz, seq, GDN_HEADS * GDN_DV), F32),
        sd((bsz, seq, GLA_HEADS * GLA_DK), F32), sd((bsz, seq, GLA_HEADS * GLA_DK), F32),
        sd((bsz, seq, GLA_HEADS * GLA_DV), F32), sd((bsz, seq, GLA_HEADS * GLA_DV), F32),
        sd((bsz, seq, LANE), F32),
    ]
    return pl.pallas_call(
        _pre_mix_kernel, out_shape=out_shape, grid=(nt, bsz), in_specs=in_specs, out_specs=out_specs,
        compiler_params=_cparams(("arbitrary", "arbitrary")), name="pre_mix",
    )(x, mods, norm_g, wts["w_in"], wts["q_norm"], wts["kv_norm"], wts["w_uq"], wts["w_kT"], wts["w_v"],
      wts["w_krT"], tabs["cq"], tabs["sq"], tabs["ckT"], tabs["skT"])


def _flash_kernel(q_ref, kT_ref, v_ref, *rest, n_groups, kv_group):
    o_ref = rest[-1]
    tq = q_ref.shape[2]
    ctx = kT_ref.shape[2] - 1
    qs = [q_ref[0, hh] for hh in range(2)]

    def attend(carry, kts, vs):
        ss = [_dot(qs[hh], kts[hh]) for hh in range(2)]
        out = []
        for hh in range(2):
            m, acc = carry[2 * hh], carry[2 * hh + 1]
            m_new = jnp.maximum(m, jnp.max(ss[hh], axis=-1, keepdims=True))
            p = jnp.exp2(ss[hh] - m_new).astype(BF16)
            alpha = jnp.exp2(m - m_new)
            out += [m_new, alpha * acc + _dot(p, vs[hh])]
        return tuple(out)

    init = (jnp.full((tq, 1), -1e30, F32), jnp.zeros((tq, HEAD_PAD), F32)) * 2
    carry = attend(init, [kT_ref[0, hh, ctx] for hh in range(2)],
                   [v_ref[0, hh, ctx * TM:(ctx + 1) * TM, :] for hh in range(2)])

    for j in range(n_groups):
        kts = [jnp.concatenate([kT_ref[0, hh, kv_group * j + c] for c in range(kv_group)], axis=1)
               for hh in range(2)]
        rows = slice(j * kv_group * TM, (j + 1) * kv_group * TM)
        carry = attend(carry, kts, [v_ref[0, hh, rows, :] for hh in range(2)])
    o_ref[0] = jnp.concatenate([carry[2 * hh + 1][:, :MLA_V] / carry[2 * hh + 1][:, MLA_V:MLA_V + 1]
                                for hh in range(2)], axis=-1)


def mla_attention(q, kT, v):
    bsz, nh, seq, _ = q.shape
    nt = seq // TM
    t_latent = seq - CTX_LEN
    kv_group = KV_GROUP if (nt - 1) % KV_GROUP == 0 else 2
    assert t_latent % TQ == 0 and (nt - 1) % kv_group == 0
    kv_specs = [pl.BlockSpec((1, 2, nt, HEAD_PAD, TM), lambda b, h, i: (b, h, 0, 0, 0)),
                pl.BlockSpec((1, 2, seq, HEAD_PAD), lambda b, h, i: (b, h, 0, 0))]
    sem = ("arbitrary", "arbitrary", "arbitrary")
    out_shape = jax.ShapeDtypeStruct((bsz, seq, nh * MLA_V), F32)
    o = pl.pallas_call(
        functools.partial(_flash_kernel, n_groups=(nt - 1) // kv_group, kv_group=kv_group),
        out_shape=out_shape,
        grid=(bsz, nh // 2, t_latent // TQ),
        in_specs=[pl.BlockSpec((1, 2, TQ, HEAD_PAD), lambda b, h, i: (b, h, i, 0))] + kv_specs,
        out_specs=pl.BlockSpec((1, TQ, 2 * MLA_V), lambda b, h, i: (b, i, h)),
        compiler_params=_cparams(sem), name="mla_attention_latent",
    )(q, kT, v)
    return pl.pallas_call(
        functools.partial(_flash_kernel, n_groups=0, kv_group=0),
        out_shape=out_shape,
        grid=(bsz, nh // 2, 1),
        in_specs=[pl.BlockSpec((1, 2, TM, HEAD_PAD), lambda b, h, i: (b, h, nt - 1, 0)),
                  pl.BlockSpec((1, 2, 1, HEAD_PAD, TM), lambda b, h, i: (b, h, nt - 1, 0, 0)),
                  pl.BlockSpec((1, 2, TM, HEAD_PAD), lambda b, h, i: (b, h, nt - 1, 0)),
                  pl.BlockSpec(memory_space=pl.ANY)],
        out_specs=pl.BlockSpec((1, TM, 2 * MLA_V), lambda b, h, i: (b, nt - 1, h)),
        input_output_aliases={3: 0},
        compiler_params=_cparams(sem), name="mla_attention_context",
    )(q, kT, v, o)


def _split3(x):
    h1 = x.astype(BF16)
    r1 = x - h1.astype(F32)
    h2 = r1.astype(BF16)
    h3 = (r1 - h2.astype(F32)).astype(BF16)
    return h1, h2, h3


def _sel_dot(sel, parts):
    return _dot(sel, parts[0]) + _dot(sel, parts[1]) + _dot(sel, parts[2])


def _dot_ta(a, b):
    return lax.dot_general(a, b, (((0,), (0,)), ((), ())), preferred_element_type=F32)


def _tri(n, d):
    r = lax.broadcasted_iota(jnp.int32, (n, n), 0)
    c = lax.broadcasted_iota(jnp.int32, (n, n), 1)
    return jnp.where(c <= r if d == 0 else c >= r, 1.0, 0.0).astype(BF16)


def _lane_head(width, per_head):
    return lax.broadcasted_iota(jnp.int32, (1, width), 1) // per_head


def _stack_heads(x, n_heads, per_head):
    lh = _lane_head(x.shape[1], per_head) % n_heads
    return jnp.concatenate([jnp.where(lh == h, x, 0.0) for h in range(n_heads)], axis=0)


def _select_heads(per_head_results, per_head):
    n_heads = len(per_head_results)
    lh = _lane_head(per_head_results[0].shape[1], per_head) % n_heads
    out = jnp.where(lh == 0, per_head_results[0], 0.0)
    for h in range(1, n_heads):
        out = out + jnp.where(lh == h, per_head_results[h], 0.0)
    return out


def _chunk_maps(n_chunks, n_latent):
    fwd = lambda n: (n + n_latent) % n_chunks
    bwd = lambda n: n_chunks - 1 - n
    return fwd, bwd


def _job_specs(widths, n_chunks, n_latent, seq, nb):
    per = RC // HALO
    last = seq // HALO - 1
    specs = []
    for cmap in _chunk_maps(n_chunks, n_latent):
        chunk = lambda w, cmap=cmap: pl.BlockSpec((nb, RC, w), lambda b, n: (b, cmap(n), 0))
        if widths[0] is not None:
            w0 = widths[0]
            specs += [chunk(w0),
                      pl.BlockSpec((nb, HALO, w0), lambda b, n, cmap=cmap: (b, jnp.maximum(cmap(n) * per - 1, 0), 0)),
                      pl.BlockSpec((nb, HALO, w0), lambda b, n, cmap=cmap: (b, jnp.minimum((cmap(n) + 1) * per, last), 0))]
        specs += [chunk(w) for w in widths[1:]]
    return specs


def _lockstep(jobs):
    live = [(job, next(job)) for job in jobs]
    while live:
        results = [[fn(a, b) for fn, a, b in req] for _, req in live]
        nxt = []
        for (job, _), res in zip(live, results):
            try:
                nxt.append((job, job.send(res)))
            except StopIteration:
                pass
        live = nxt


def _gdn_job(d, c, n_chunks, n_latent, qkv, prev, nxt, misc, conv_w_ref, esel_ref,
             alog_ref, dtb_ref, gsum_ref, s_ref, s_idx, store_o):
    hw = GDN_HEADS * GDN_DK
    heads = range(GDN_HEADS)
    prev_ok = (c != 0) & (c != n_latent)
    next_ok = (c != n_latent - 1) & (c != n_chunks - 1)
    x = _silu(_conv3(jnp.concatenate([prev, qkv, nxt], axis=0), conv_w_ref, slice(None), prev_ok, next_ok))
    gsum = gsum_ref[...]
    q, k, v = x[:, :hw], x[:, hw:2 * hw], x[:, 2 * hw:]
    sq = [q * q, k * k]
    hi = [a.astype(BF16) for a in sq]
    lo = [(a - h.astype(F32)).astype(BF16) for a, h in zip(sq, hi)]
    r = yield [(_dot, a, gsum) for a in hi + lo]
    q = q * lax.rsqrt(r[0] + r[2] + NORM_EPS) * GDN_DK ** -0.5
    k = k * lax.rsqrt(r[1] + r[3] + NORM_EPS)

    r = yield [(_dot, p, esel_ref[d]) for p in _split3(misc)]
    ab = r[0] + r[1] + r[2]
    g = -jnp.exp(alog_ref[d]) * _softplus(ab[:, :hw] + dtb_ref[d])
    beta = jax.nn.sigmoid(ab[:, hw:])
    tri = _tri(RC, d)
    r = yield [(_dot, tri, p) for p in _split3(g)]
    gc = r[0] + r[1] + r[2]
    total = gc[RC - 1:RC] if d == 0 else gc[0:1]
    eg = jnp.exp(gc)
    kb = k * beta
    rhs = jnp.concatenate([v * beta, kb * eg], axis=1)
    qd = q * eg
    kend = k * jnp.exp(total - gc)

    bd = lambda m: _stack_heads(m, GDN_HEADS, RC)
    ks = bd(k).astype(BF16)
    kk, qk = yield [(_dot_t, kb.astype(BF16), ks), (_dot_t, q.astype(BF16), ks)]
    ii = lax.broadcasted_iota(jnp.int32, (RC, hw), 0)
    jj = lax.broadcasted_iota(jnp.int32, (RC, hw), 1) % RC
    gc_j = jnp.sum(jnp.where(ii == jj, gc, 0.0), axis=0, keepdims=True)
    decay = jnp.exp(jnp.minimum(gc - gc_j, 0.0))
    strict = ii > jj if d == 0 else ii < jj
    lmat = jnp.where(strict, kk * decay, 0.0)
    amat = jnp.where(strict | (ii == jj), qk * decay, 0.0)

    n_sq = int(math.log2(RC)) - 1
    p = lmat
    t = jnp.where(ii == jj, 1.0, 0.0) - lmat
    (p,) = yield [(_dot, p, bd(p))]
    for _ in range(n_sq - 1):
        (r,) = yield [(_dot, jnp.concatenate([p, t], axis=0), bd(p))]
        p, t = r[:RC], t + r[RC:]
    (r,) = yield [(_dot, t, bd(p))]
    t = t + r
    (uw,) = yield [(_dot, t.astype(BF16), bd(rhs).astype(BF16))]
    u, w = uw[:, :hw], uw[:, hw:]

    s = s_ref[s_idx]
    (ws,) = yield [(_dot, jnp.concatenate([w, qd], axis=0).astype(BF16), s.astype(BF16))]
    v_new = u - ws[:RC]
    o_intra, kv = yield [(_dot, amat.astype(BF16), bd(v_new).astype(BF16)), (_dot_ta, kend, v_new)]
    store_o(ws[RC:] + o_intra)
    rh = lax.broadcasted_iota(jnp.int32, (hw, hw), 0) // GDN_DK
    ch = lax.broadcasted_iota(jnp.int32, (hw, hw), 1) // GDN_DV
    s_ref[s_idx] = s * jnp.exp(total) + jnp.where(rh == ch, kv, 0.0)


def _sel_dot_r(parts, sel):
    return _dot(parts[0], sel) + _dot(parts[1], sel) + _dot(parts[2], sel)


def _transpose(x):
    n, m = x.shape
    xp = jnp.concatenate([x, jnp.zeros((LANE - n, m), x.dtype)], axis=0) if n < LANE else x
    return xp.T[:, :n]


def _store_at(ref, bi):
    def store(val):
        ref[bi] = val
    return store


def _gla_job(d, q, k, v, misc, wgk_ref, bgk_ref, st_ref, s_idx, store_o):
    heads = range(GLA_HEADS)
    q = q * GLA_DK ** -0.5
    (gk,) = yield [(_dot, misc.astype(BF16), wgk_ref[d])]
    la = -_softplus(-(gk + bgk_ref[d])) * (1.0 / GLA_NORMALIZER)
    tri = _tri(RC, d)
    r = yield [(_dot, tri, p) for p in _split3(la)]
    parts = _split3(r[0] + r[1] + r[2])
    b = parts[0].astype(F32) + parts[1].astype(F32) + parts[2].astype(F32)
    blast = b[RC - 1:RC] if d == 0 else b[0:1]

    ii = lax.broadcasted_iota(jnp.int32, (RC, GLA_HEADS * RC), 0)
    jj = lax.broadcasted_iota(jnp.int32, (RC, GLA_HEADS * RC), 1) % RC
    tt = lax.broadcasted_iota(jnp.int32, (RC, RC), 0)
    tc = lax.broadcasted_iota(jnp.int32, (RC, RC), 1)
    n_levels = int(math.log2(RC))
    sels = []
    for ls in range(n_levels):
        start = (tt >> (ls + 1)) << (ls + 1)
        mid = start + ((1 << ls) - 1 if d == 0 else (1 << ls))
        sels.append(jnp.where(tc == mid, 1.0, 0.0).astype(BF16))
    r = yield [(_dot, sel, p) for sel in sels for p in parts]
    lhs, rhs, pairs = [q.astype(BF16)], [_stack_heads(k, GLA_HEADS, GLA_DK).astype(BF16)], [ii == jj]
    for ls in range(n_levels):
        s = 1 << ls
        ref = r[3 * ls] + r[3 * ls + 1] + r[3 * ls + 2]
        lhs.append((q * jnp.exp(jnp.minimum(b - ref, 0.0))).astype(BF16))
        rhs.append(_stack_heads(k * jnp.exp(jnp.minimum(ref - b, 0.0)), GLA_HEADS, GLA_DK).astype(BF16))
        same = (ii >> (ls + 1)) == (jj >> (ls + 1))
        i_hi = (ii & (2 * s - 1)) >= s
        j_hi = (jj & (2 * s - 1)) >= s
        pairs.append(same & (i_hi & ~j_hi if d == 0 else ~i_hi & j_hi))
    r = yield [(_dot_t, a, bb) for a, bb in zip(lhs, rhs)]
    amat = jnp.where(pairs[0], r[0], 0.0)
    for pair, prod in zip(pairs[1:], r[1:]):
        amat = amat + jnp.where(pair, prod, 0.0)

    st = st_ref[s_idx]
    kend = k * jnp.exp(blast - b)
    o_intra, o_inter, vk = yield [
        (_dot, amat.astype(BF16), _stack_heads(v, GLA_HEADS, GLA_DV).astype(BF16)),
        (_dot_t, (q * jnp.exp(b)).astype(BF16), st.astype(BF16)),
        (_dot_ta, v, kend)]
    store_o(o_intra + o_inter)
    rh = lax.broadcasted_iota(jnp.int32, st.shape, 0) // GLA_DV
    ch = lax.broadcasted_iota(jnp.int32, st.shape, 1) // GLA_DK
    st_ref[s_idx] = st * jnp.exp(blast) + jnp.where(rh == ch, vk, 0.0)


def _mixers_kernel(*refs, n_chunks, n_latent):
    (qkv_f, prev_f, next_f, misc_f, qkv_r, prev_r, next_r, misc_r,
     lq_f, lk_f, lv_f, lq_r, lk_r, lv_r,
     conv_w, esel, alog, dtb, gsum, wgk, bgk,
     gdn_f, gdn_r, gla_f, gla_r, s_ref, st_ref) = refs
    n = pl.program_id(1)

    @pl.when(n == 0)
    def _():
        s_ref[...] = jnp.zeros_like(s_ref)
        st_ref[...] = jnp.zeros_like(st_ref)

    fwd, bwd = _chunk_maps(n_chunks, n_latent)
    gdn_consts = (conv_w, esel, alog, dtb, gsum, s_ref)
    jobs = []
    for bi in range(qkv_f.shape[0]):
        jobs.append(_gdn_job(0, fwd(n), n_chunks, n_latent, qkv_f[bi], prev_f[bi], next_f[bi], misc_f[bi],
                             *gdn_consts, 2 * bi, _store_at(gdn_f, bi)))
        jobs.append(_gdn_job(1, bwd(n), n_chunks, n_latent, qkv_r[bi], prev_r[bi], next_r[bi], misc_r[bi],
                             *gdn_consts, 2 * bi + 1, _store_at(gdn_r, bi)))
    for bi in range(lq_f.shape[0]):
        jobs.append(_gla_job(0, lq_f[bi], lk_f[bi], lv_f[bi], misc_f[bi], wgk, bgk, st_ref, 2 * bi,
                             _store_at(gla_f, bi)))
        jobs.append(_gla_job(1, lq_r[bi], lk_r[bi], lv_r[bi], misc_r[bi], wgk, bgk, st_ref, 2 * bi + 1,
                             _store_at(gla_r, bi)))
    _lockstep(jobs)


def recurrent_mixers(gqkv, misc, lq, lk, lv, wts):
    bsz, seq, hk = lq.shape
    hv = lv.shape[2]
    hw = GDN_HEADS * GDN_DV
    n_chunks = seq // RC
    n_latent = (seq - CTX_LEN) // RC
    nb = RNB if bsz % RNB == 0 else 1
    const = lambda *shape: pl.BlockSpec(shape, lambda b, n: (0,) * len(shape))
    fwd, bwd = _chunk_maps(n_chunks, n_latent)
    out = lambda w: jax.ShapeDtypeStruct((bsz, seq, w), F32)
    out_spec = lambda w, cmap: pl.BlockSpec((nb, RC, w), lambda b, n: (b, cmap(n), 0))
    return pl.pallas_call(
        functools.partial(_mixers_kernel, n_chunks=n_chunks, n_latent=n_latent),
        out_shape=[out(hw), out(hw), out(hv), out(hv)],
        grid=(bsz // nb, n_chunks),
        in_specs=_job_specs((GDN_QKV, LANE), n_chunks, n_latent, seq, nb)
        + _job_specs((None, hk, hk, hv), n_chunks, n_latent, seq, nb)
        + [const(3, GDN_QKV), const(2, LANE, 2 * hw), const(2, 1, hw), const(2, 1, hw), const(hw, hw),
           const(2, LANE, hk), const(2, 1, hk)],
        out_specs=[out_spec(hw, fwd), out_spec(hw, bwd), out_spec(hv, fwd), out_spec(hv, bwd)],
        scratch_shapes=[pltpu.VMEM((2 * nb, hw, hw), F32), pltpu.VMEM((2 * nb, hv, hk), F32)],
        compiler_params=_cparams(("arbitrary", "arbitrary")),
        name="recurrent_mixers",
    )(gqkv, gqkv, gqkv, misc, gqkv, gqkv, gqkv, misc, lq, lk, lv, lq, lk, lv,
      wts["gdn_conv_w"], wts["gdn_esel"], wts["gdn_alog"], wts["gdn_dtb"], wts["gsum"],
      wts["gla_wgk"], wts["gla_bgk"])


def _group_mean_sq(x, gmat):
    xsq = x * x
    hi = xsq.astype(BF16)
    lo = (xsq - hi.astype(F32)).astype(BF16)
    return _dot(hi, gmat) + _dot(lo, gmat)


def _post_mix_kernel(x_ref, mod_ref, mla_ref, gdn_f_ref, gdn_r_ref, gz_ref, gla_f_ref, gla_r_ref, lg_ref,
                     gmat_ref, gdn_g_ref, gla_g_ref, w_out_ref, post_g_ref, o_ref):
    gmat = gmat_ref[...]
    gdn = gdn_f_ref[0] + gdn_r_ref[0]
    gdn = gdn * lax.rsqrt(_group_mean_sq(gdn, gmat) + NORM_EPS) * gdn_g_ref[...] * _silu(gz_ref[0])
    gla = gla_f_ref[0] + gla_r_ref[0]
    gla = gla * lax.rsqrt(_group_mean_sq(gla, gmat) + NORM_EPS) * gla_g_ref[...] * _silu(lg_ref[0])
    merged = jnp.concatenate([mla_ref[0], gdn, gla], axis=-1).astype(BF16)
    y = _dot(merged, w_out_ref[...])
    gate = mod_ref[0, 2:3, :]
    o_ref[0] = x_ref[0] + gate * _rms(y, post_g_ref[...])


def post_mix(x, mods, mla_o, gdn_f, gdn_r, gz, gla_f, gla_r, lg, wts, post_g):
    bsz, seq, d = x.shape
    nt = seq // TM
    const = lambda *shape: pl.BlockSpec(shape, lambda b, i: (0,) * len(shape))
    tile = lambda w: pl.BlockSpec((1, TM, w), lambda b, i: (b, i, 0))
    hw = GDN_HEADS * GDN_DV
    return pl.pallas_call(
        _post_mix_kernel,
        out_shape=jax.ShapeDtypeStruct(x.shape, F32),
        grid=(bsz, nt),
        in_specs=[tile(d),
                  pl.BlockSpec((1, N_MOD, d), lambda b, i: (jnp.where(i == nt - 1, bsz, b), 0, 0)),
                  tile(MLA_HEADS * MLA_V), tile(hw), tile(hw), tile(hw), tile(hw), tile(hw), tile(hw),
                  const(hw, hw), const(1, hw), const(1, hw), const(D_MIX, d), const(1, d)],
        out_specs=tile(d),
        compiler_params=_cparams(("arbitrary", "arbitrary")),
        name="post_mix",
    )(x, mods, mla_o, gdn_f, gdn_r, gz, gla_f, gla_r, lg, wts["gmat"], wts["gdn_norm"], wts["gla_norm"],
      wts["w_out"], post_g)


def _conv3(z, w_ref, cols, prev_ok, next_ok, pad=0.0):
    rows = z.shape[0]
    n = rows - 2 * HALO
    z = jnp.concatenate([jnp.where(prev_ok, z[:HALO], pad), z[HALO:HALO + n],
                         jnp.where(next_ok, z[HALO + n:], pad)], axis=0)
    zm = pltpu.roll(z, 1, axis=0)[HALO:HALO + n]
    zp = pltpu.roll(z, rows - 1, axis=0)[HALO:HALO + n]
    zc = z[HALO:HALO + n]
    return zm * w_ref[0:1, cols] + zc * w_ref[1:2, cols] + zp * w_ref[2:3, cols]


def _ffn_kernel(x_ref, xp_ref, xn_ref, mod_ref, pre_g_ref, w_in_ref, b_in_ref, cw_ref, cb_ref,
                w_out_ref, post_g_ref, o_ref):
    i = pl.program_id(1)
    nt = pl.num_programs(1)
    prev_ok = (i > 0) & (i < nt - 1)
    next_ok = i < nt - 2
    x = x_ref[0]
    xx = jnp.concatenate([xp_ref[0], x, xn_ref[0]], axis=0)
    shift = mod_ref[0, 3:4, :]
    scale = mod_ref[0, 4:5, :]
    hb = (_rms(xx, pre_g_ref[...]) * (1.0 + scale) + shift).astype(BF16)
    n_chunks = FFN_HIDDEN // FFN_CHUNK
    cols = [(slice(j * FFN_CHUNK, (j + 1) * FFN_CHUNK),
             slice(FFN_HIDDEN + j * FFN_CHUNK, FFN_HIDDEN + (j + 1) * FFN_CHUNK)) for j in range(n_chunks)]
    acts = []
    for ca, cg in cols:
        za, zg = _dot(hb, w_in_ref[:, ca]), _dot(hb, w_in_ref[:, cg])
        bias = lambda c: cb_ref[:, c] + b_in_ref[:, c] * (cw_ref[0:1, c] + cw_ref[1:2, c] + cw_ref[2:3, c])
        a = _conv3(za, cw_ref, ca, prev_ok, next_ok, -b_in_ref[:, ca]) + bias(ca)
        g = _conv3(zg, cw_ref, cg, prev_ok, next_ok, -b_in_ref[:, cg]) + bias(cg)
        acts.append((a * _silu(g)).astype(BF16))
    acc = _dot(jnp.concatenate(acts, axis=1), w_out_ref[...])
    gate = mod_ref[0, 5:6, :]
    o_ref[0] = x + gate * _rms(acc, post_g_ref[...])


def conv_ffn(x, mods, pre_g, wts, post_g):
    bsz, seq, d = x.shape
    nt = seq // TM
    nb = seq // HALO
    per = TM // HALO
    const = lambda *shape: pl.BlockSpec(shape, lambda b, i: (0,) * len(shape))
    tile = pl.BlockSpec((1, TM, d), lambda b, i: (b, i, 0))
    return pl.pallas_call(
        _ffn_kernel,
        out_shape=jax.ShapeDtypeStruct(x.shape, F32),
        grid=(bsz, nt),
        in_specs=[tile,
                  pl.BlockSpec((1, HALO, d), lambda b, i: (b, jnp.maximum(i * per - 1, 0), 0)),
                  pl.BlockSpec((1, HALO, d), lambda b, i: (b, jnp.minimum((i + 1) * per, nb - 1), 0)),
                  pl.BlockSpec((1, N_MOD, d), lambda b, i: (jnp.where(i == nt - 1, bsz, b), 0, 0)),
                  const(1, d), const(d, 2 * FFN_HIDDEN), const(1, 2 * FFN_HIDDEN),
                  const(3, 2 * FFN_HIDDEN), const(1, 2 * FFN_HIDDEN), const(FFN_HIDDEN, d), const(1, d)],
        out_specs=tile,
        compiler_params=_cparams(("arbitrary", "arbitrary")),
        name="conv_ffn",
    )(x, x, x, mods, pre_g, wts["ffn_w_in"], wts["ffn_b_in"], wts["ffn_conv_w"], wts["ffn_conv_b"],
      wts["ffn_w_out"], post_g)


def _rope_tables(t_latent):
    n = MLA_ROPE // 4
    t = jnp.arange(t_latent)
    row = (t // GRID_W).astype(F32)
    col = (t % GRID_W).astype(F32)
    inv = ROPE_THETA ** (-jnp.arange(n, dtype=F32) / n)
    ang = jnp.stack([row[:, None] * inv, col[:, None] * inv], axis=1)
    cos, sin = jnp.cos(ang), jnp.sin(ang)
    c32 = jnp.stack([cos, cos], axis=2).reshape(t_latent, MLA_ROPE)
    s32 = jnp.stack([-sin, sin], axis=2).reshape(t_latent, MLA_ROPE)
    c32 = jnp.concatenate([c32, jnp.ones((CTX_LEN, MLA_ROPE), F32)], axis=0)
    s32 = jnp.concatenate([s32, jnp.zeros((CTX_LEN, MLA_ROPE), F32)], axis=0)
    seq = CTX_LEN + t_latent
    qscale = (MLA_NOPE + MLA_ROPE) ** -0.5 * math.log2(math.e)
    pad = HEAD_PAD - MLA_NOPE - MLA_ROPE
    cq = jnp.concatenate([jnp.ones((seq, MLA_NOPE), F32), c32, jnp.zeros((seq, pad), F32)], axis=1) * qscale
    sq = jnp.concatenate([jnp.zeros((seq, MLA_NOPE), F32), s32, jnp.zeros((seq, pad), F32)], axis=1) * qscale
    return {"cq": jnp.tile(cq, (1, MLA_HEADS)), "sq": jnp.tile(sq, (1, MLA_HEADS)),
            "ckT": c32.T, "skT": s32.T}


_ROPE_PARTNER = np.arange(MLA_ROPE) ^ (MLA_ROPE // 4)


def _prep_layer(i, w_in, mla_q_norm, mla_w_uq, mla_kv_norm, mla_w_ukv, gdn_conv_w, gdn_a_log, gdn_dt_bias,
                gdn_norm, gla_w_gk, gla_b_gk, gla_norm, w_out, ffn_w_in, ffn_b_in, ffn_conv_w, ffn_conv_b,
                ffn_w_out):
    o = IN_OFFS
    w = w_in[i]
    d = w.shape[0]
    piece = lambda k: w[:, o[k]:o[k + 1]]
    misc_pad = LANE - (MLA_ROPE + 4 * GDN_HEADS + 2 * GLA_GATE_RANK)
    w_in_p = jnp.concatenate(
        [piece(0), piece(1), piece(3), piece(4), piece(7), piece(8), piece(9), piece(10),
         piece(2), piece(5), piece(6), piece(11), jnp.zeros((d, misc_pad), F32)], axis=1).astype(BF16)
    kr = piece(2)
    w_krT = jnp.concatenate([kr, kr[:, _ROPE_PARTNER]], axis=1).T.astype(BF16)

    hq = MLA_NOPE + MLA_ROPE
    uq = mla_w_uq[i].reshape(MLA_Q_RANK, MLA_HEADS, hq)
    zq = lambda n: jnp.zeros((MLA_Q_RANK, MLA_HEADS, n), F32)
    plain = jnp.concatenate([uq, zq(HEAD_PAD - hq)], axis=2)
    partner = jnp.concatenate([zq(MLA_NOPE), uq[:, :, MLA_NOPE:][:, :, _ROPE_PARTNER], zq(HEAD_PAD - hq)], axis=2)
    w_uq = jnp.concatenate([plain.reshape(MLA_Q_RANK, -1), partner.reshape(MLA_Q_RANK, -1)], axis=1).astype(BF16)

    ukv = mla_w_ukv[i].reshape(MLA_KV_RANK, MLA_HEADS, MLA_NOPE + MLA_V)
    w_kT = ukv[:, :, :MLA_NOPE].reshape(MLA_KV_RANK, -1).T.astype(BF16)
    w_v = jnp.concatenate([ukv[:, :, MLA_NOPE:], jnp.zeros((MLA_KV_RANK, MLA_HEADS, HEAD_PAD - MLA_V), F32)],
                          axis=2).reshape(MLA_KV_RANK, -1).astype(BF16)

    hw = GDN_HEADS * GDN_DV
    lane_head = np.arange(hw) // GDN_DV
    same_head = (lane_head[:, None] == lane_head[None, :]).astype(np.float32)
    esel = np.zeros((2, LANE, 2 * hw), np.float32)
    for dd in range(2):
        for h in range(GDN_HEADS):
            esel[dd, M_A + dd * GDN_HEADS + h, h * GDN_DV:(h + 1) * GDN_DV] = 1.0
            esel[dd, M_B + dd * GDN_HEADS + h, hw + h * GDN_DV:hw + (h + 1) * GDN_DV] = 1.0
    wgk = jnp.zeros((2, LANE, GLA_HEADS * GLA_DK), F32)
    for dd in range(2):
        r0 = M_LR + dd * GLA_GATE_RANK
        wgk = wgk.at[dd, r0:r0 + GLA_GATE_RANK].set(gla_w_gk[i, dd])
    return {
        "w_in": w_in_p, "w_krT": w_krT, "w_uq": w_uq, "w_kT": w_kT, "w_v": w_v,
        "q_norm": mla_q_norm[i][None], "kv_norm": mla_kv_norm[i][None],
        "gdn_conv_w": gdn_conv_w[i], "gdn_esel": jnp.asarray(esel).astype(BF16),
        "gdn_alog": jnp.repeat(gdn_a_log[i], GDN_DV, axis=1)[:, None, :],
        "gdn_dtb": jnp.repeat(gdn_dt_bias[i], GDN_DV, axis=1)[:, None, :],
        "gsum": jnp.asarray(same_head).astype(BF16),
        "gla_wgk": wgk.astype(BF16), "gla_bgk": gla_b_gk[i][:, None, :],
        "gmat": jnp.asarray(same_head / GDN_DV).astype(BF16), "gdn_norm": jnp.tile(gdn_norm[i], GDN_HEADS)[None],
        "gla_norm": jnp.tile(gla_norm[i], GLA_HEADS)[None], "w_out": w_out[i].astype(BF16),
        "ffn_w_in": ffn_w_in[i].astype(BF16), "ffn_b_in": ffn_b_in[i][None], "ffn_conv_w": ffn_conv_w[i],
        "ffn_conv_b": ffn_conv_b[i][None], "ffn_w_out": ffn_w_out[i].astype(BF16),
    }


def kernel(x, c, ctx, c_ctx, w_ada, b_ada, norm_mix_pre, norm_mix_post, norm_ffn_pre, norm_ffn_post,
           w_in, mla_q_norm, mla_w_uq, mla_kv_norm, mla_w_ukv, gdn_conv_w, gdn_a_log, gdn_dt_bias,
           gdn_norm, gla_w_gk, gla_b_gk, gla_norm, w_out, ffn_w_in, ffn_b_in, ffn_conv_w, ffn_conv_b,
           ffn_w_out):
    bsz, t_latent, d = x.shape
    assert ctx.shape[1] == CTX_LEN and t_latent % TM == 0 and t_latent % GRID_W == 0
    depth = w_ada.shape[0]
    xs = jnp.concatenate([x, ctx], axis=1)
    mod_rows = -(-(bsz + 1) // SUBLANE) * SUBLANE
    cvec = jnp.concatenate([c, c_ctx[None], jnp.zeros((mod_rows - bsz - 1, d), F32)], axis=0)
    mods_all = ada_modulation(cvec, w_ada, b_ada).reshape(depth, mod_rows, N_MOD, d)
    tabs = _rope_tables(t_latent)
    for i in range(depth):
        mods = mods_all[i]
        wts = _prep_layer(i, w_in, mla_q_norm, mla_w_uq, mla_kv_norm, mla_w_ukv, gdn_conv_w, gdn_a_log,
                          gdn_dt_bias, gdn_norm, gla_w_gk, gla_b_gk, gla_norm, w_out, ffn_w_in, ffn_b_in,
                          ffn_conv_w, ffn_conv_b, ffn_w_out)
        q, kT, v, gqkv, gz, lq, lk, lv, lg, misc = pre_mix(xs, mods, norm_mix_pre[i][None], wts, tabs)
        mla_o = mla_attention(q, kT, v)
        gdn_f, gdn_r, gla_f, gla_r = recurrent_mixers(gqkv, misc, lq, lk, lv, wts)
        xs = post_mix(xs, mods, mla_o, gdn_f, gdn_r, gz, gla_f, gla_r, lg, wts, norm_mix_post[i][None])
        xs = conv_ffn(xs, mods, norm_ffn_pre[i][None], wts, norm_ffn_post[i][None])
    return xs[:, :t_latent]
```

```python
import functools
import math

import numpy as np
import jax
import jax.numpy as jnp
from jax import lax
from jax.experimental import pallas as pl
from jax.experimental.pallas import tpu as pltpu

F32 = jnp.float32
BF16 = jnp.bfloat16

D_MODEL = 1024
DEPTH = 4
GRID_W = 64
CTX_LEN = 256
N_MOD = 6
NORM_EPS = 1e-6

MLA_HEADS = 8
MLA_Q_RANK = 384
MLA_KV_RANK = 256
MLA_NOPE = 64
MLA_ROPE = 32
MLA_V = 64
ROPE_THETA = 10000.0

GDN_HEADS = 4
GDN_DK = 64
GDN_DV = 64
GDN_CHUNK = 64

GLA_HEADS = 4
GLA_DK = 32
GLA_DV = 64
GLA_GATE_RANK = 16
GLA_NORMALIZER = 16.0
GLA_CHUNK = 16

FFN_HIDDEN = 2560
D_MIX = MLA_HEADS * MLA_V + GDN_HEADS * GDN_DV + GLA_HEADS * GLA_DV
GDN_QKV = GDN_HEADS * (2 * GDN_DK + GDN_DV)
IN_SIZES = (MLA_Q_RANK, MLA_KV_RANK, MLA_ROPE,
            GDN_QKV, GDN_HEADS * GDN_DV, 2 * GDN_HEADS, 2 * GDN_HEADS,
            GLA_HEADS * GLA_DK, GLA_HEADS * GLA_DK, GLA_HEADS * GLA_DV, GLA_HEADS * GLA_DV,
            2 * GLA_GATE_RANK)
IN_OFFS = tuple(int(s) for s in np.cumsum((0,) + IN_SIZES))

LANE = 128
SUBLANE = 8
TM = CTX_LEN
TQ = 2 * TM
KV_GROUP = 8
HALO = SUBLANE
HEAD_PAD = LANE
FFN_CHUNK = 512
RC = 64
RNB = 4
assert RC == GDN_DK
VMEM_LIMIT = 56 * 1024 * 1024

P_CQ = 0
P_CKV = P_CQ + MLA_Q_RANK
P_GQKV = P_CKV + MLA_KV_RANK
P_GZ = P_GQKV + GDN_QKV
P_LQ = P_GZ + GDN_HEADS * GDN_DV
P_LK = P_LQ + GLA_HEADS * GLA_DK
P_LV = P_LK + GLA_HEADS * GLA_DK
P_LG = P_LV + GLA_HEADS * GLA_DV
P_MISC = P_LG + GLA_HEADS * GLA_DV
P_TOTAL = P_MISC + LANE
M_KR = 0
M_A = MLA_ROPE
M_B = M_A + 2 * GDN_HEADS
M_LR = M_B + 2 * GDN_HEADS


def _cparams(sem, vmem=VMEM_LIMIT):
    return pltpu.CompilerParams(dimension_semantics=sem, vmem_limit_bytes=vmem)


def _rms(x, g):
    return x * lax.rsqrt(jnp.mean(x * x, axis=-1, keepdims=True) + NORM_EPS) * g


def _silu(x):
    return x * jax.nn.sigmoid(x)


def _softplus(x):
    return jnp.maximum(x, 0.0) + jnp.log1p(jnp.exp(-jnp.abs(x)))


def _dot(a, b):
    return jnp.dot(a, b, preferred_element_type=F32)


def _dot_t(a, b):
    return lax.dot_general(a, b, (((1,), (1,)), ((), ())), preferred_element_type=F32)


def _ada_kernel(c_ref, w_ref, b_ref, o_ref):
    o_ref[0] = _dot(_silu(c_ref[...]), w_ref[0]) + b_ref[0]


def ada_modulation(cvec, w_ada, b_ada):
    depth, d, n = w_ada.shape
    rows = cvec.shape[0]
    tn = n // 4
    return pl.pallas_call(
        _ada_kernel,
        out_shape=jax.ShapeDtypeStruct((depth, rows, n), F32),
        grid=(depth, n // tn),
        in_specs=[pl.BlockSpec((rows, d), lambda l, j: (0, 0)),
                  pl.BlockSpec((1, d, tn), lambda l, j: (l, 0, j)),
                  pl.BlockSpec((1, 1, tn), lambda l, j: (l, 0, j))],
        out_specs=pl.BlockSpec((1, rows, tn), lambda l, j: (l, 0, j)),
        compiler_params=_cparams(("arbitrary", "arbitrary")),
        name="ada_modulation",
    )(cvec, w_ada, b_ada.reshape(depth, 1, n))


def _pre_mix_kernel(x_ref, mod_ref, g_ref, w_in_ref, qn_ref, kvn_ref, w_uq_ref, w_kT_ref, w_v_ref,
                    w_krT_ref, cq_ref, sq_ref, ckT_ref, skT_ref,
                    q_ref, kT_ref, v_ref, gqkv_ref, gz_ref, lq_ref, lk_ref, lv_ref, lg_ref, misc_ref):
    x = x_ref[0]
    shift = mod_ref[0, 0:1, :]
    scale = mod_ref[0, 1:2, :]
    h = _rms(x, g_ref[...]) * (1.0 + scale) + shift
    hb = h.astype(BF16)
    p = _dot(hb, w_in_ref[...])

    gqkv_ref[0] = p[:, P_GQKV:P_GZ]
    gz_ref[0] = p[:, P_GZ:P_LQ]
    lq_ref[0] = p[:, P_LQ:P_LK]
    lk_ref[0] = p[:, P_LK:P_LV]
    lv_ref[0] = p[:, P_LV:P_LG]
    lg_ref[0] = p[:, P_LG:P_MISC]
    misc_ref[0] = p[:, P_MISC:P_TOTAL]

    cqn = _rms(p[:, P_CQ:P_CKV], qn_ref[...]).astype(BF16)
    ckvn = _rms(p[:, P_CKV:P_GQKV], kvn_ref[...]).astype(BF16)
    nq = MLA_HEADS * HEAD_PAD
    qf = _dot(cqn, w_uq_ref[...])
    q = qf[:, :nq] * cq_ref[...] + qf[:, nq:] * sq_ref[...]
    vf = _dot(ckvn, w_v_ref[...])
    one_lane = lax.broadcasted_iota(jnp.int32, (1, HEAD_PAD), 1) == MLA_V
    kTn = _dot_t(w_kT_ref[...], ckvn)
    krT = _dot_t(w_krT_ref[...], hb)
    kr = (krT[:MLA_ROPE] * ckT_ref[...] + krT[MLA_ROPE:] * skT_ref[...]).astype(BF16)
    zpad = jnp.zeros((HEAD_PAD - MLA_NOPE - MLA_ROPE, kr.shape[1]), BF16)
    for hd in range(MLA_HEADS):
        q_ref[0, hd] = q[:, hd * HEAD_PAD:(hd + 1) * HEAD_PAD].astype(BF16)
        vh = vf[:, hd * HEAD_PAD:(hd + 1) * HEAD_PAD]
        v_ref[0, hd] = jnp.where(one_lane, 1.0, vh).astype(BF16)
        kT_ref[0, hd, 0, 0:MLA_NOPE, :] = kTn[hd * MLA_NOPE:(hd + 1) * MLA_NOPE].astype(BF16)
        kT_ref[0, hd, 0, MLA_NOPE:MLA_NOPE + MLA_ROPE, :] = kr
        kT_ref[0, hd, 0, MLA_NOPE + MLA_ROPE:, :] = zpad


def pre_mix(x, mods, norm_g, wts, tabs):
    bsz, seq, d = x.shape
    nt = seq // TM
    const = lambda *shape: pl.BlockSpec(shape, lambda i, b: (0,) * len(shape))
    tile = lambda w: pl.BlockSpec((1, TM, w), lambda i, b: (b, i, 0))
    nq = MLA_HEADS * HEAD_PAD
    in_specs = [
        tile(d),
        pl.BlockSpec((1, N_MOD, d), lambda i, b: (jnp.where(i == nt - 1, bsz, b), 0, 0)),
        const(1, d),
        const(d, P_TOTAL),
        const(1, MLA_Q_RANK),
        const(1, MLA_KV_RANK),
        const(MLA_Q_RANK, 2 * nq),
        const(MLA_HEADS * MLA_NOPE, MLA_KV_RANK),
        const(MLA_KV_RANK, nq),
        const(2 * MLA_ROPE, d),
        pl.BlockSpec((TM, nq), lambda i, b: (i, 0)),
        pl.BlockSpec((TM, nq), lambda i, b: (i, 0)),
        pl.BlockSpec((MLA_ROPE, TM), lambda i, b: (0, i)),
        pl.BlockSpec((MLA_ROPE, TM), lambda i, b: (0, i)),
    ]
    head_rows = pl.BlockSpec((1, MLA_HEADS, TM, HEAD_PAD), lambda i, b: (b, 0, i, 0))
    out_specs = [
        head_rows,
        pl.BlockSpec((1, MLA_HEADS, 1, HEAD_PAD, TM), lambda i, b: (b, 0, i, 0, 0)),
        head_rows,
        tile(GDN_QKV), tile(GDN_HEADS * GDN_DV),
        tile(GLA_HEADS * GLA_DK), tile(GLA_HEADS * GLA_DK),
        tile(GLA_HEADS * GLA_DV), tile(GLA_HEADS * GLA_DV),
        tile(LANE),
    ]
    sd = jax.ShapeDtypeStruct
    out_shape = [
        sd((bsz, MLA_HEADS, seq, HEAD_PAD), BF16),
        sd((bsz, MLA_HEADS, nt, HEAD_PAD, TM), BF16),
        sd((bsz, MLA_HEADS, seq, HEAD_PAD), BF16),
        sd((bsz, seq, GDN_QKV), F32), sd((bsz, seq, GDN_HEADS * GDN_DV), F32),
        sd((bsz, seq, GLA_HEADS * GLA_DK), F32), sd((bsz, seq, GLA_HEADS * GLA_DK), F32),
        sd((bsz, seq, GLA_HEADS * GLA_DV), F32), sd((bsz, seq, GLA_HEADS * GLA_DV), F32),
        sd((bsz, seq, LANE), F32),
    ]
    return pl.pallas_call(
        _pre_mix_kernel, out_shape=out_shape, grid=(nt, bsz), in_specs=in_specs, out_specs=out_specs,
        compiler_params=_cparams(("arbitrary", "arbitrary")), name="pre_mix",
    )(x, mods, norm_g, wts["w_in"], wts["q_norm"], wts["kv_norm"], wts["w_uq"], wts["w_kT"], wts["w_v"],
      wts["w_krT"], tabs["cq"], tabs["sq"], tabs["ckT"], tabs["skT"])


def _flash_kernel(q_ref, kT_ref, v_ref, o_ref, *, n_groups, kv_group):
    tq = q_ref.shape[2]
    ctx = kT_ref.shape[2] - 1
    qs = [q_ref[0, hh] for hh in range(2)]

    def attend(carry, kts, vs):
        ss = [_dot(qs[hh], kts[hh]) for hh in range(2)]
        out = []
        for hh in range(2):
            m, acc = carry[2 * hh], carry[2 * hh + 1]
            m_new = jnp.maximum(m, jnp.max(ss[hh], axis=-1, keepdims=True))
            p = jnp.exp2(ss[hh] - m_new).astype(BF16)
            alpha = jnp.exp2(m - m_new)
            out += [m_new, alpha * acc + _dot(p, vs[hh])]
        return tuple(out)

    init = (jnp.full((tq, 1), -1e30, F32), jnp.zeros((tq, HEAD_PAD), F32)) * 2
    carry = attend(init, [kT_ref[0, hh, ctx] for hh in range(2)],
                   [v_ref[0, hh, ctx * TM:(ctx + 1) * TM, :] for hh in range(2)])

    for j in range(n_groups):
        kts = [jnp.concatenate([kT_ref[0, hh, kv_group * j + c] for c in range(kv_group)], axis=1)
               for hh in range(2)]
        rows = slice(j * kv_group * TM, (j + 1) * kv_group * TM)
        carry = attend(carry, kts, [v_ref[0, hh, rows, :] for hh in range(2)])
    o_ref[0] = jnp.concatenate([carry[2 * hh + 1][:, :MLA_V] / carry[2 * hh + 1][:, MLA_V:MLA_V + 1]
                                for hh in range(2)], axis=-1)


def mla_attention(q, kT, v):
    bsz, nh, seq, _ = q.shape
    nt = seq // TM
    t_latent = seq - CTX_LEN
    kv_group = KV_GROUP if (nt - 1) % KV_GROUP == 0 else 2
    assert t_latent % TQ == 0 and (nt - 1) % kv_group == 0
    kv_specs = [pl.BlockSpec((1, 2, nt, HEAD_PAD, TM), lambda b, h, i: (b, h, 0, 0, 0)),
                pl.BlockSpec((1, 2, seq, HEAD_PAD), lambda b, h, i: (b, h, 0, 0))]
    sem = ("arbitrary", "arbitrary", "arbitrary")
    o_latent = pl.pallas_call(
        functools.partial(_flash_kernel, n_groups=(nt - 1) // kv_group, kv_group=kv_group),
        out_shape=jax.ShapeDtypeStruct((bsz, t_latent, nh * MLA_V), F32),
        grid=(bsz, nh // 2, t_latent // TQ),
        in_specs=[pl.BlockSpec((1, 2, TQ, HEAD_PAD), lambda b, h, i: (b, h, i, 0))] + kv_specs,
        out_specs=pl.BlockSpec((1, TQ, 2 * MLA_V), lambda b, h, i: (b, i, h)),
        compiler_params=_cparams(sem), name="mla_attention_latent",
    )(q, kT, v)
    o_context = pl.pallas_call(
        functools.partial(_flash_kernel, n_groups=0, kv_group=0),
        out_shape=jax.ShapeDtypeStruct((bsz, CTX_LEN, nh * MLA_V), F32),
        grid=(bsz, nh // 2, 1),
        in_specs=[pl.BlockSpec((1, 2, TM, HEAD_PAD), lambda b, h, i: (b, h, nt - 1, 0)),
                  pl.BlockSpec((1, 2, 1, HEAD_PAD, TM), lambda b, h, i: (b, h, nt - 1, 0, 0)),
                  pl.BlockSpec((1, 2, TM, HEAD_PAD), lambda b, h, i: (b, h, nt - 1, 0))],
        out_specs=pl.BlockSpec((1, TM, 2 * MLA_V), lambda b, h, i: (b, 0, h)),
        compiler_params=_cparams(sem), name="mla_attention_context",
    )(q, kT, v)
    return o_latent, o_context


def _split3(x):
    h1 = x.astype(BF16)
    r1 = x - h1.astype(F32)
    h2 = r1.astype(BF16)
    h3 = (r1 - h2.astype(F32)).astype(BF16)
    return h1, h2, h3


def _dot_ta(a, b):
    return lax.dot_general(a, b, (((0,), (0,)), ((), ())), preferred_element_type=F32)


def _tri(n, d):
    r = lax.broadcasted_iota(jnp.int32, (n, n), 0)
    c = lax.broadcasted_iota(jnp.int32, (n, n), 1)
    return jnp.where(c <= r if d == 0 else c >= r, 1.0, 0.0).astype(BF16)


def _lane_head(width, per_head):
    return lax.broadcasted_iota(jnp.int32, (1, width), 1) // per_head


def _stack_heads(x, n_heads, per_head):
    lh = _lane_head(x.shape[1], per_head) % n_heads
    return jnp.concatenate([jnp.where(lh == h, x, 0.0) for h in range(n_heads)], axis=0)


def _chunk_maps(n_chunks, n_latent):
    fwd = lambda n: (n + n_latent) % n_chunks
    bwd = lambda n: n_chunks - 1 - n
    return fwd, bwd


def _job_specs(widths, n_chunks, n_latent, seq, nb):
    per = RC // HALO
    last = seq // HALO - 1
    specs = []
    for cmap in _chunk_maps(n_chunks, n_latent):
        chunk = lambda w, cmap=cmap: pl.BlockSpec((nb, RC, w), lambda b, n: (b, cmap(n), 0))
        if widths[0] is not None:
            w0 = widths[0]
            specs += [chunk(w0),
                      pl.BlockSpec((nb, HALO, w0), lambda b, n, cmap=cmap: (b, jnp.maximum(cmap(n) * per - 1, 0), 0)),
                      pl.BlockSpec((nb, HALO, w0), lambda b, n, cmap=cmap: (b, jnp.minimum((cmap(n) + 1) * per, last), 0))]
        specs += [chunk(w) for w in widths[1:]]
    return specs


def _lockstep(jobs):
    live = [(job, next(job)) for job in jobs]
    while live:
        groups = {}
        for ji, (_, reqs) in enumerate(live):
            for ri, (fn, a, b) in enumerate(reqs):
                groups.setdefault((fn, id(b)), []).append((ji, ri, a, b))
        results = [[None] * len(reqs) for _, reqs in live]
        for (fn, _), members in groups.items():
            if fn in (_dot, _dot_t) and len(members) > 1:
                out = fn(jnp.concatenate([a for _, _, a, _ in members], axis=0), members[0][3])
                row = 0
                for ji, ri, a, _ in members:
                    results[ji][ri] = out[row:row + a.shape[0]]
                    row += a.shape[0]
            else:
                for ji, ri, a, b in members:
                    results[ji][ri] = fn(a, b)
        nxt = []
        for (job, _), res in zip(live, results):
            try:
                nxt.append((job, job.send(res)))
            except StopIteration:
                pass
        live = nxt


def _gdn_job(d, c, n_chunks, n_latent, qkv, prev, nxt, misc, conv_w_ref, esel,
             alog_ref, dtb_ref, gsum, s_ref, s_idx, store_o):
    hw = GDN_HEADS * GDN_DK
    prev_ok = (c != 0) & (c != n_latent)
    next_ok = (c != n_latent - 1) & (c != n_chunks - 1)
    x = _silu(_conv3(jnp.concatenate([prev, qkv, nxt], axis=0), conv_w_ref, slice(None), prev_ok, next_ok))
    q, k, v = x[:, :hw], x[:, hw:2 * hw], x[:, 2 * hw:]
    sq = [q * q, k * k]
    hi = [a.astype(BF16) for a in sq]
    lo = [(a - h.astype(F32)).astype(BF16) for a, h in zip(sq, hi)]
    r = yield [(_dot, a, gsum) for a in hi + lo]
    q = q * lax.rsqrt(r[0] + r[2] + NORM_EPS) * GDN_DK ** -0.5
    k = k * lax.rsqrt(r[1] + r[3] + NORM_EPS)

    lane = lax.broadcasted_iota(jnp.int32, (1, LANE), 1)
    gates = jnp.where((lane >= M_A) & (lane < M_B),
                      -jnp.exp(alog_ref[...]) * _softplus(misc + dtb_ref[...]), jax.nn.sigmoid(misc))
    r = yield [(_dot, p, esel[d]) for p in _split3(gates)]
    gb = r[0] + r[1] + r[2]
    g, beta = gb[:, :hw], gb[:, hw:]
    tri = _tri(RC, d)
    r = yield [(_dot, tri, p) for p in _split3(g)]
    gc = r[0] + r[1] + r[2]
    total = gc[RC - 1:RC] if d == 0 else gc[0:1]
    eg = jnp.exp(gc)
    kb = k * beta
    rhs = jnp.concatenate([v * beta, kb * eg], axis=1)
    qd = q * eg
    kend = k * jnp.exp(total - gc)

    bd = lambda m: _stack_heads(m, GDN_HEADS, RC)
    ks = bd(k).astype(BF16)
    kk, qk = yield [(_dot_t, kb.astype(BF16), ks), (_dot_t, q.astype(BF16), ks)]
    ii = lax.broadcasted_iota(jnp.int32, (RC, hw), 0)
    jj = lax.broadcasted_iota(jnp.int32, (RC, hw), 1) % RC
    gc_j = jnp.sum(jnp.where(ii == jj, gc, 0.0), axis=0, keepdims=True)
    decay = jnp.exp(jnp.minimum(gc - gc_j, 0.0))
    strict = ii > jj if d == 0 else ii < jj
    lmat = jnp.where(strict, kk * decay, 0.0)
    amat = jnp.where(strict | (ii == jj), qk * decay, 0.0)

    n_sq = int(math.log2(RC)) - 1
    p = lmat
    t = jnp.where(ii == jj, 1.0, 0.0) - lmat
    (p,) = yield [(_dot, p, bd(p))]
    for _ in range(n_sq - 1):
        (r,) = yield [(_dot, jnp.concatenate([p, t], axis=0), bd(p))]
        p, t = r[:RC], t + r[RC:]
    (r,) = yield [(_dot, t, bd(p))]
    t = t + r
    (uw,) = yield [(_dot, t.astype(BF16), bd(rhs).astype(BF16))]
    u, w = uw[:, :hw], uw[:, hw:]

    s = s_ref[s_idx]
    (ws,) = yield [(_dot, jnp.concatenate([w, qd], axis=0).astype(BF16), s.astype(BF16))]
    v_new = u - ws[:RC]
    o_intra, kv = yield [(_dot, amat.astype(BF16), bd(v_new).astype(BF16)), (_dot_ta, kend, v_new)]
    store_o(ws[RC:] + o_intra)
    rh = lax.broadcasted_iota(jnp.int32, (hw, hw), 0) // GDN_DK
    ch = lax.broadcasted_iota(jnp.int32, (hw, hw), 1) // GDN_DV
    s_ref[s_idx] = s * jnp.exp(total) + jnp.where(rh == ch, kv, 0.0)


def _store_at(ref, bi):
    def store(val):
        ref[bi] = val
    return store


def _block_mid_rows(b, ls, d):
    n, width = b.shape
    s = 1 << ls
    off = s - 1 if d == 0 else s
    if 2 * s >= SUBLANE:
        return jnp.concatenate([jnp.broadcast_to(b[m + off:m + off + 1], (2 * s, width))
                                for m in range(0, n, 2 * s)], axis=0)
    pos = lax.broadcasted_iota(jnp.int32, (n, 1), 0) & (2 * s - 1)
    out = None
    for r in range(2 * s):
        shift = (r - off) % n
        rolled = b if shift == 0 else pltpu.roll(b, shift, axis=0)
        out = rolled if out is None else jnp.where(pos == r, rolled, out)
    return out


def _gla_job(d, q, k, v, misc, wgk, bgk_ref, st_ref, s_idx, store_o):
    q = q * GLA_DK ** -0.5
    (gk,) = yield [(_dot, misc.astype(BF16), wgk[d])]
    la = -_softplus(-(gk + bgk_ref[d])) * (1.0 / GLA_NORMALIZER)
    tri = _tri(RC, d)
    r = yield [(_dot, tri, p) for p in _split3(la)]
    b = r[0] + r[1] + r[2]
    blast = b[RC - 1:RC] if d == 0 else b[0:1]

    ii = lax.broadcasted_iota(jnp.int32, (RC, GLA_HEADS * RC), 0)
    jj = lax.broadcasted_iota(jnp.int32, (RC, GLA_HEADS * RC), 1) % RC
    n_levels = int(math.log2(RC))
    lhs, rhs, pairs = [q.astype(BF16)], [_stack_heads(k, GLA_HEADS, GLA_DK).astype(BF16)], [ii == jj]
    for ls in range(n_levels):
        s = 1 << ls
        ref = _block_mid_rows(b, ls, d)
        lhs.append((q * jnp.exp(jnp.minimum(b - ref, 0.0))).astype(BF16))
        rhs.append(_stack_heads(k * jnp.exp(jnp.minimum(ref - b, 0.0)), GLA_HEADS, GLA_DK).astype(BF16))
        same = (ii >> (ls + 1)) == (jj >> (ls + 1))
        i_hi = (ii & (2 * s - 1)) >= s
        j_hi = (jj & (2 * s - 1)) >= s
        pairs.append(same & (i_hi & ~j_hi if d == 0 else ~i_hi & j_hi))
    r = yield [(_dot_t, a, bb) for a, bb in zip(lhs, rhs)]
    amat = jnp.where(pairs[0], r[0], 0.0)
    for pair, prod in zip(pairs[1:], r[1:]):
        amat = amat + jnp.where(pair, prod, 0.0)

    st = st_ref[s_idx]
    kend = k * jnp.exp(blast - b)
    o_intra, o_inter, vk = yield [
        (_dot, amat.astype(BF16), _stack_heads(v, GLA_HEADS, GLA_DV).astype(BF16)),
        (_dot_t, (q * jnp.exp(b)).astype(BF16), st.astype(BF16)),
        (_dot_ta, v, kend)]
    store_o(o_intra + o_inter)
    rh = lax.broadcasted_iota(jnp.int32, st.shape, 0) // GLA_DV
    ch = lax.broadcasted_iota(jnp.int32, st.shape, 1) // GLA_DK
    st_ref[s_idx] = st * jnp.exp(blast) + jnp.where(rh == ch, vk, 0.0)


def _mixers_kernel(*refs, n_chunks, n_latent):
    (qkv_f, prev_f, next_f, misc_f, qkv_r, prev_r, next_r, misc_r,
     lq_f, lk_f, lv_f, lq_r, lk_r, lv_r,
     conv_w, esel, alog, dtb, gsum, wgk, bgk,
     gdn_f, gdn_r, gla_f, gla_r, s_ref, st_ref) = refs
    n = pl.program_id(1)

    @pl.when(n == 0)
    def _():
        s_ref[...] = jnp.zeros_like(s_ref)
        st_ref[...] = jnp.zeros_like(st_ref)

    fwd, bwd = _chunk_maps(n_chunks, n_latent)
    esel, wgk = [esel[0], esel[1]], [wgk[0], wgk[1]]
    gdn_consts = (conv_w, esel, alog, dtb, gsum[...], s_ref)
    jobs = []
    for bi in range(qkv_f.shape[0]):
        jobs.append(_gdn_job(0, fwd(n), n_chunks, n_latent, qkv_f[bi], prev_f[bi], next_f[bi], misc_f[bi],
                             *gdn_consts, 2 * bi, _store_at(gdn_f, bi)))
        jobs.append(_gdn_job(1, bwd(n), n_chunks, n_latent, qkv_r[bi], prev_r[bi], next_r[bi], misc_r[bi],
                             *gdn_consts, 2 * bi + 1, _store_at(gdn_r, bi)))
    for bi in range(lq_f.shape[0]):
        jobs.append(_gla_job(0, lq_f[bi], lk_f[bi], lv_f[bi], misc_f[bi], wgk, bgk, st_ref, 2 * bi,
                             _store_at(gla_f, bi)))
        jobs.append(_gla_job(1, lq_r[bi], lk_r[bi], lv_r[bi], misc_r[bi], wgk, bgk, st_ref, 2 * bi + 1,
                             _store_at(gla_r, bi)))
    _lockstep(jobs)


def recurrent_mixers(gqkv, misc, lq, lk, lv, wts):
    bsz, seq, hk = lq.shape
    hv = lv.shape[2]
    hw = GDN_HEADS * GDN_DV
    n_chunks = seq // RC
    n_latent = (seq - CTX_LEN) // RC
    nb = RNB if bsz % RNB == 0 else 1
    const = lambda *shape: pl.BlockSpec(shape, lambda b, n: (0,) * len(shape))
    fwd, bwd = _chunk_maps(n_chunks, n_latent)
    out = lambda w: jax.ShapeDtypeStruct((bsz, seq, w), F32)
    out_spec = lambda w, cmap: pl.BlockSpec((nb, RC, w), lambda b, n: (b, cmap(n), 0))
    return pl.pallas_call(
        functools.partial(_mixers_kernel, n_chunks=n_chunks, n_latent=n_latent),
        out_shape=[out(hw), out(hw), out(hv), out(hv)],
        grid=(bsz // nb, n_chunks),
        in_specs=_job_specs((GDN_QKV, LANE), n_chunks, n_latent, seq, nb)
        + _job_specs((None, hk, hk, hv), n_chunks, n_latent, seq, nb)
        + [const(3, GDN_QKV), const(2, LANE, 2 * hw), const(1, LANE), const(1, LANE), const(hw, hw),
           const(2, LANE, hk), const(2, 1, hk)],
        out_specs=[out_spec(hw, fwd), out_spec(hw, bwd), out_spec(hv, fwd), out_spec(hv, bwd)],
        scratch_shapes=[pltpu.VMEM((2 * nb, hw, hw), F32), pltpu.VMEM((2 * nb, hv, hk), F32)],
        compiler_params=_cparams(("arbitrary", "arbitrary")),
        name="recurrent_mixers",
    )(gqkv, gqkv, gqkv, misc, gqkv, gqkv, gqkv, misc, lq, lk, lv, lq, lk, lv,
      wts["gdn_conv_w"], wts["gdn_esel"], wts["gdn_alog"], wts["gdn_dtb"], wts["gsum"],
      wts["gla_wgk"], wts["gla_bgk"])


def _group_mean_sq(x, gmat):
    xsq = x * x
    hi = xsq.astype(BF16)
    lo = (xsq - hi.astype(F32)).astype(BF16)
    return _dot(hi, gmat) + _dot(lo, gmat)


def _post_mix_kernel(x_ref, mod_ref, mla_l_ref, mla_c_ref, gdn_f_ref, gdn_r_ref, gz_ref, gla_f_ref, gla_r_ref,
                     lg_ref, gmat_ref, gdn_g_ref, gla_g_ref, w_out_ref, post_g_ref, o_ref):
    is_ctx = pl.program_id(1) == pl.num_programs(1) - 1
    mla = jnp.where(is_ctx, mla_c_ref[0], mla_l_ref[0])
    gmat = gmat_ref[...]
    gdn = gdn_f_ref[0] + gdn_r_ref[0]
    gdn = gdn * lax.rsqrt(_group_mean_sq(gdn, gmat) + NORM_EPS) * gdn_g_ref[...] * _silu(gz_ref[0])
    gla = gla_f_ref[0] + gla_r_ref[0]
    gla = gla * lax.rsqrt(_group_mean_sq(gla, gmat) + NORM_EPS) * gla_g_ref[...] * _silu(lg_ref[0])
    merged = jnp.concatenate([mla, gdn, gla], axis=-1).astype(BF16)
    y = _dot(merged, w_out_ref[...])
    gate = mod_ref[0, 2:3, :]
    o_ref[0] = x_ref[0] + gate * _rms(y, post_g_ref[...])


def post_mix(x, mods, mla_l, mla_c, gdn_f, gdn_r, gz, gla_f, gla_r, lg, wts, post_g):
    bsz, seq, d = x.shape
    nt = seq // TM
    const = lambda *shape: pl.BlockSpec(shape, lambda b, i: (0,) * len(shape))
    tile = lambda w: pl.BlockSpec((1, TM, w), lambda b, i: (b, i, 0))
    hw = GDN_HEADS * GDN_DV
    hm = MLA_HEADS * MLA_V
    return pl.pallas_call(
        _post_mix_kernel,
        out_shape=jax.ShapeDtypeStruct(x.shape, F32),
        grid=(bsz, nt),
        in_specs=[tile(d),
                  pl.BlockSpec((1, N_MOD, d), lambda b, i: (jnp.where(i == nt - 1, bsz, b), 0, 0)),
                  pl.BlockSpec((1, TM, hm), lambda b, i: (b, jnp.minimum(i, nt - 2), 0)),
                  pl.BlockSpec((1, TM, hm), lambda b, i: (b, 0, 0)),
                  tile(hw), tile(hw), tile(hw), tile(hw), tile(hw), tile(hw),
                  const(hw, hw), const(1, hw), const(1, hw), const(D_MIX, d), const(1, d)],
        out_specs=tile(d),
        compiler_params=_cparams(("arbitrary", "arbitrary")),
        name="post_mix",
    )(x, mods, mla_l, mla_c, gdn_f, gdn_r, gz, gla_f, gla_r, lg, wts["gmat"], wts["gdn_norm"], wts["gla_norm"],
      wts["w_out"], post_g)


def _conv3(z, w_ref, cols, prev_ok, next_ok, pad=0.0):
    rows = z.shape[0]
    n = rows - 2 * HALO
    z = jnp.concatenate([jnp.where(prev_ok, z[:HALO], pad), z[HALO:HALO + n],
                         jnp.where(next_ok, z[HALO + n:], pad)], axis=0)
    zm = pltpu.roll(z, 1, axis=0)[HALO:HALO + n]
    zp = pltpu.roll(z, rows - 1, axis=0)[HALO:HALO + n]
    zc = z[HALO:HALO + n]
    return zm * w_ref[0:1, cols] + zc * w_ref[1:2, cols] + zp * w_ref[2:3, cols]


def _ffn_kernel(x_ref, xp_ref, xn_ref, mod_ref, pre_g_ref, w_in_ref, b_in_ref, cw_ref, cb_ref,
                w_out_ref, post_g_ref, o_ref):
    i = pl.program_id(1)
    nt = pl.num_programs(1)
    prev_ok = (i > 0) & (i < nt - 1)
    next_ok = i < nt - 2
    x = x_ref[0]
    xx = jnp.concatenate([xp_ref[0], x, xn_ref[0]], axis=0)
    shift = mod_ref[0, 3:4, :]
    scale = mod_ref[0, 4:5, :]
    hb = (_rms(xx, pre_g_ref[...]) * (1.0 + scale) + shift).astype(BF16)
    n_chunks = FFN_HIDDEN // FFN_CHUNK
    cols = [(slice(j * FFN_CHUNK, (j + 1) * FFN_CHUNK),
             slice(FFN_HIDDEN + j * FFN_CHUNK, FFN_HIDDEN + (j + 1) * FFN_CHUNK)) for j in range(n_chunks)]
    acts = []
    for ca, cg in cols:
        za, zg = _dot(hb, w_in_ref[:, ca]), _dot(hb, w_in_ref[:, cg])
        bias = lambda c: cb_ref[:, c] + b_in_ref[:, c] * (cw_ref[0:1, c] + cw_ref[1:2, c] + cw_ref[2:3, c])
        a = _conv3(za, cw_ref, ca, prev_ok, next_ok, -b_in_ref[:, ca]) + bias(ca)
        g = _conv3(zg, cw_ref, cg, prev_ok, next_ok, -b_in_ref[:, cg]) + bias(cg)
        acts.append((a * _silu(g)).astype(BF16))
    acc = _dot(jnp.concatenate(acts, axis=1), w_out_ref[...])
    gate = mod_ref[0, 5:6, :]
    o_ref[0] = x + gate * _rms(acc, post_g_ref[...])


def conv_ffn(x, mods, pre_g, wts, post_g):
    bsz, seq, d = x.shape
    nt = seq // TM
    nb = seq // HALO
    per = TM // HALO
    const = lambda *shape: pl.BlockSpec(shape, lambda b, i: (0,) * len(shape))
    tile = pl.BlockSpec((1, TM, d), lambda b, i: (b, i, 0))
    return pl.pallas_call(
        _ffn_kernel,
        out_shape=jax.ShapeDtypeStruct(x.shape, F32),
        grid=(bsz, nt),
        in_specs=[tile,
                  pl.BlockSpec((1, HALO, d), lambda b, i: (b, jnp.maximum(i * per - 1, 0), 0)),
                  pl.BlockSpec((1, HALO, d), lambda b, i: (b, jnp.minimum((i + 1) * per, nb - 1), 0)),
                  pl.BlockSpec((1, N_MOD, d), lambda b, i: (jnp.where(i == nt - 1, bsz, b), 0, 0)),
                  const(1, d), const(d, 2 * FFN_HIDDEN), const(1, 2 * FFN_HIDDEN),
                  const(3, 2 * FFN_HIDDEN), const(1, 2 * FFN_HIDDEN), const(FFN_HIDDEN, d), const(1, d)],
        out_specs=tile,
        compiler_params=_cparams(("arbitrary", "arbitrary")),
        name="conv_ffn",
    )(x, x, x, mods, pre_g, wts["ffn_w_in"], wts["ffn_b_in"], wts["ffn_conv_w"], wts["ffn_conv_b"],
      wts["ffn_w_out"], post_g)


def _rope_tables(t_latent):
    n = MLA_ROPE // 4
    t = jnp.arange(t_latent)
    row = (t // GRID_W).astype(F32)
    col = (t % GRID_W).astype(F32)
    inv = ROPE_THETA ** (-jnp.arange(n, dtype=F32) / n)
    ang = jnp.stack([row[:, None] * inv, col[:, None] * inv], axis=1)
    cos, sin = jnp.cos(ang), jnp.sin(ang)
    c32 = jnp.stack([cos, cos], axis=2).reshape(t_latent, MLA_ROPE)
    s32 = jnp.stack([-sin, sin], axis=2).reshape(t_latent, MLA_ROPE)
    c32 = jnp.concatenate([c32, jnp.ones((CTX_LEN, MLA_ROPE), F32)], axis=0)
    s32 = jnp.concatenate([s32, jnp.zeros((CTX_LEN, MLA_ROPE), F32)], axis=0)
    seq = CTX_LEN + t_latent
    qscale = (MLA_NOPE + MLA_ROPE) ** -0.5 * math.log2(math.e)
    pad = HEAD_PAD - MLA_NOPE - MLA_ROPE
    cq = jnp.concatenate([jnp.ones((seq, MLA_NOPE), F32), c32, jnp.zeros((seq, pad), F32)], axis=1) * qscale
    sq = jnp.concatenate([jnp.zeros((seq, MLA_NOPE), F32), s32, jnp.zeros((seq, pad), F32)], axis=1) * qscale
    return {"cq": jnp.tile(cq, (1, MLA_HEADS)), "sq": jnp.tile(sq, (1, MLA_HEADS)),
            "ckT": c32.T, "skT": s32.T}


_ROPE_PARTNER = np.arange(MLA_ROPE) ^ (MLA_ROPE // 4)


def _prep_layer(i, w_in, mla_q_norm, mla_w_uq, mla_kv_norm, mla_w_ukv, gdn_conv_w, gdn_a_log, gdn_dt_bias,
                gdn_norm, gla_w_gk, gla_b_gk, gla_norm, w_out, ffn_w_in, ffn_b_in, ffn_conv_w, ffn_conv_b,
                ffn_w_out):
    o = IN_OFFS
    w = w_in[i]
    d = w.shape[0]
    piece = lambda k: w[:, o[k]:o[k + 1]]
    misc_pad = LANE - (M_LR + 2 * GLA_GATE_RANK)
    w_in_p = jnp.concatenate(
        [piece(0), piece(1), piece(3), piece(4), piece(7), piece(8), piece(9), piece(10),
         piece(2), piece(5), piece(6), piece(11), jnp.zeros((d, misc_pad), F32)], axis=1).astype(BF16)
    kr = piece(2)
    w_krT = jnp.concatenate([kr, kr[:, _ROPE_PARTNER]], axis=1).T.astype(BF16)

    hq = MLA_NOPE + MLA_ROPE
    uq = mla_w_uq[i].reshape(MLA_Q_RANK, MLA_HEADS, hq)
    zq = lambda n: jnp.zeros((MLA_Q_RANK, MLA_HEADS, n), F32)
    plain = jnp.concatenate([uq, zq(HEAD_PAD - hq)], axis=2)
    partner = jnp.concatenate([zq(MLA_NOPE), uq[:, :, MLA_NOPE:][:, :, _ROPE_PARTNER], zq(HEAD_PAD - hq)], axis=2)
    w_uq = jnp.concatenate([plain.reshape(MLA_Q_RANK, -1), partner.reshape(MLA_Q_RANK, -1)], axis=1).astype(BF16)

    ukv = mla_w_ukv[i].reshape(MLA_KV_RANK, MLA_HEADS, MLA_NOPE + MLA_V)
    w_kT = ukv[:, :, :MLA_NOPE].reshape(MLA_KV_RANK, -1).T.astype(BF16)
    w_v = jnp.concatenate([ukv[:, :, MLA_NOPE:], jnp.zeros((MLA_KV_RANK, MLA_HEADS, HEAD_PAD - MLA_V), F32)],
                          axis=2).reshape(MLA_KV_RANK, -1).astype(BF16)

    hw = GDN_HEADS * GDN_DV
    lane_head = np.arange(hw) // GDN_DV
    same_head = (lane_head[:, None] == lane_head[None, :]).astype(np.float32)
    esel = np.zeros((2, LANE, 2 * hw), np.float32)
    for dd in range(2):
        for h in range(GDN_HEADS):
            esel[dd, M_A + dd * GDN_HEADS + h, h * GDN_DV:(h + 1) * GDN_DV] = 1.0
            esel[dd, M_B + dd * GDN_HEADS + h, hw + h * GDN_DV:hw + (h + 1) * GDN_DV] = 1.0
    wgk = jnp.zeros((2, LANE, GLA_HEADS * GLA_DK), F32)
    for dd in range(2):
        r0 = M_LR + dd * GLA_GATE_RANK
        wgk = wgk.at[dd, r0:r0 + GLA_GATE_RANK].set(gla_w_gk[i, dd])
    return {
        "w_in": w_in_p, "w_krT": w_krT, "w_uq": w_uq, "w_kT": w_kT, "w_v": w_v,
        "q_norm": mla_q_norm[i][None], "kv_norm": mla_kv_norm[i][None],
        "gdn_conv_w": gdn_conv_w[i], "gdn_esel": jnp.asarray(esel).astype(BF16),
        "gdn_alog": jnp.zeros((1, LANE), F32).at[0, M_A:M_B].set(gdn_a_log[i].reshape(-1)),
        "gdn_dtb": jnp.zeros((1, LANE), F32).at[0, M_A:M_B].set(gdn_dt_bias[i].reshape(-1)),
        "gsum": jnp.asarray(same_head).astype(BF16),
        "gla_wgk": wgk.astype(BF16), "gla_bgk": gla_b_gk[i][:, None, :],
        "gmat": jnp.asarray(same_head / GDN_DV).astype(BF16), "gdn_norm": jnp.tile(gdn_norm[i], GDN_HEADS)[None],
        "gla_norm": jnp.tile(gla_norm[i], GLA_HEADS)[None], "w_out": w_out[i].astype(BF16),
        "ffn_w_in": ffn_w_in[i].astype(BF16), "ffn_b_in": ffn_b_in[i][None], "ffn_conv_w": ffn_conv_w[i],
        "ffn_conv_b": ffn_conv_b[i][None], "ffn_w_out": ffn_w_out[i].astype(BF16),
    }


def kernel(x, c, ctx, c_ctx, w_ada, b_ada, norm_mix_pre, norm_mix_post, norm_ffn_pre, norm_ffn_post,
           w_in, mla_q_norm, mla_w_uq, mla_kv_norm, mla_w_ukv, gdn_conv_w, gdn_a_log, gdn_dt_bias,
           gdn_norm, gla_w_gk, gla_b_gk, gla_norm, w_out, ffn_w_in, ffn_b_in, ffn_conv_w, ffn_conv_b,
           ffn_w_out):
    bsz, t_latent, d = x.shape
    assert ctx.shape[1] == CTX_LEN and t_latent % TM == 0 and t_latent % GRID_W == 0
    depth = w_ada.shape[0]
    xs = jnp.concatenate([x, ctx], axis=1)
    mod_rows = -(-(bsz + 1) // SUBLANE) * SUBLANE
    cvec = jnp.concatenate([c, c_ctx[None], jnp.zeros((mod_rows - bsz - 1, d), F32)], axis=0)
    mods_all = ada_modulation(cvec, w_ada, b_ada).reshape(depth, mod_rows, N_MOD, d)
    tabs = _rope_tables(t_latent)
    for i in range(depth):
        mods = mods_all[i]
        wts = _prep_layer(i, w_in, mla_q_norm, mla_w_uq, mla_kv_norm, mla_w_ukv, gdn_conv_w, gdn_a_log,
                          gdn_dt_bias, gdn_norm, gla_w_gk, gla_b_gk, gla_norm, w_out, ffn_w_in, ffn_b_in,
                          ffn_conv_w, ffn_conv_b, ffn_w_out)
        q, kT, v, gqkv, gz, lq, lk, lv, lg, misc = pre_mix(xs, mods, norm_mix_pre[i][None], wts, tabs)
        mla_l, mla_c = mla_attention(q, kT, v)
        gdn_f, gdn_r, gla_f, gla_r = recurrent_mixers(gqkv, misc, lq, lk, lv, wts)
        xs = post_mix(xs, mods, mla_l, mla_c, gdn_f, gdn_r, gz, gla_f, gla_r, lg, wts, norm_mix_post[i][None])
        xs = conv_ffn(xs, mods, norm_ffn_pre[i][None], wts, norm_ffn_post[i][None])
    return xs[:, :t_latent]
```

```python
import functools
import math

import numpy as np
import jax
import jax.numpy as jnp
from jax import lax
from jax.experimental import pallas as pl
from jax.experimental.pallas import tpu as pltpu

F32 = jnp.float32
BF16 = jnp.bfloat16

D_MODEL = 1024
DEPTH = 4
GRID_W = 64
CTX_LEN = 256
N_MOD = 6
NORM_EPS = 1e-6

MLA_HEADS = 8
MLA_Q_RANK = 384
MLA_KV_RANK = 256
MLA_NOPE = 64
MLA_ROPE = 32
MLA_V = 64
ROPE_THETA = 10000.0

GDN_HEADS = 4
GDN_DK = 64
GDN_DV = 64
GDN_CHUNK = 64

GLA_HEADS = 4
GLA_DK = 32
GLA_DV = 64
GLA_GATE_RANK = 16
GLA_NORMALIZER = 16.0
GLA_CHUNK = 16

FFN_HIDDEN = 2560
D_MIX = MLA_HEADS * MLA_V + GDN_HEADS * GDN_DV + GLA_HEADS * GLA_DV
GDN_QKV = GDN_HEADS * (2 * GDN_DK + GDN_DV)
IN_SIZES = (MLA_Q_RANK, MLA_KV_RANK, MLA_ROPE,
            GDN_QKV, GDN_HEADS * GDN_DV, 2 * GDN_HEADS, 2 * GDN_HEADS,
            GLA_HEADS * GLA_DK, GLA_HEADS * GLA_DK, GLA_HEADS * GLA_DV, GLA_HEADS * GLA_DV,
            2 * GLA_GATE_RANK)
IN_OFFS = tuple(int(s) for s in np.cumsum((0,) + IN_SIZES))

LANE = 128
SUBLANE = 8
TM = CTX_LEN
TQ = 2 * TM
KV_GROUP = 8
HALO = SUBLANE
HEAD_PAD = LANE
FFN_CHUNK = 512
RC = 64
RNB = 8
assert RC == GDN_DK
VMEM_LIMIT = 56 * 1024 * 1024

P_CQ = 0
P_CKV = P_CQ + MLA_Q_RANK
P_GQKV = P_CKV + MLA_KV_RANK
P_GZ = P_GQKV + GDN_QKV
P_LQ = P_GZ + GDN_HEADS * GDN_DV
P_LK = P_LQ + GLA_HEADS * GLA_DK
P_LV = P_LK + GLA_HEADS * GLA_DK
P_LG = P_LV + GLA_HEADS * GLA_DV
P_MISC = P_LG + GLA_HEADS * GLA_DV
P_TOTAL = P_MISC + LANE
M_KR = 0
M_A = MLA_ROPE
M_B = M_A + 2 * GDN_HEADS
M_LR = M_B + 2 * GDN_HEADS


def _cparams(sem, vmem=VMEM_LIMIT):
    return pltpu.CompilerParams(dimension_semantics=sem, vmem_limit_bytes=vmem)


def _rms(x, g):
    return x * lax.rsqrt(jnp.mean(x * x, axis=-1, keepdims=True) + NORM_EPS) * g


def _silu(x):
    return x * jax.nn.sigmoid(x)


def _softplus(x):
    return jnp.maximum(x, 0.0) + jnp.log1p(jnp.exp(-jnp.abs(x)))


def _dot(a, b):
    return jnp.dot(a, b, preferred_element_type=F32)


def _dot_t(a, b):
    return lax.dot_general(a, b, (((1,), (1,)), ((), ())), preferred_element_type=F32)


def _ada_kernel(c_ref, w_ref, b_ref, o_ref):
    o_ref[0] = _dot(_silu(c_ref[...]), w_ref[0]) + b_ref[0]


def ada_modulation(cvec, w_ada, b_ada):
    depth, d, n = w_ada.shape
    rows = cvec.shape[0]
    tn = n // 4
    return pl.pallas_call(
        _ada_kernel,
        out_shape=jax.ShapeDtypeStruct((depth, rows, n), F32),
        grid=(depth, n // tn),
        in_specs=[pl.BlockSpec((rows, d), lambda l, j: (0, 0)),
                  pl.BlockSpec((1, d, tn), lambda l, j: (l, 0, j)),
                  pl.BlockSpec((1, 1, tn), lambda l, j: (l, 0, j))],
        out_specs=pl.BlockSpec((1, rows, tn), lambda l, j: (l, 0, j)),
        compiler_params=_cparams(("arbitrary", "arbitrary")),
        name="ada_modulation",
    )(cvec, w_ada, b_ada.reshape(depth, 1, n))


def _pre_mix_kernel(x_ref, mod_ref, g_ref, w_in_ref, qn_ref, kvn_ref, w_uq_ref, w_kT_ref, w_v_ref,
                    w_krT_ref, cq_ref, sq_ref, ckT_ref, skT_ref,
                    q_ref, kT_ref, v_ref, gqkv_ref, gz_ref, lq_ref, lk_ref, lv_ref, lg_ref, misc_ref):
    x = x_ref[0]
    shift = mod_ref[0, 0:1, :]
    scale = mod_ref[0, 1:2, :]
    h = _rms(x, g_ref[...]) * (1.0 + scale) + shift
    hb = h.astype(BF16)
    p = _dot(hb, w_in_ref[...])

    gqkv_ref[0] = p[:, P_GQKV:P_GZ]
    gz_ref[0] = p[:, P_GZ:P_LQ]
    lq_ref[0] = p[:, P_LQ:P_LK]
    lk_ref[0] = p[:, P_LK:P_LV]
    lv_ref[0] = p[:, P_LV:P_LG]
    lg_ref[0] = p[:, P_LG:P_MISC]
    misc_ref[0] = p[:, P_MISC:P_TOTAL]

    cqn = _rms(p[:, P_CQ:P_CKV], qn_ref[...]).astype(BF16)
    ckvn = _rms(p[:, P_CKV:P_GQKV], kvn_ref[...]).astype(BF16)
    nq = MLA_HEADS * HEAD_PAD
    qf = _dot(cqn, w_uq_ref[...])
    q = qf[:, :nq] * cq_ref[...] + qf[:, nq:] * sq_ref[...]
    vf = _dot(ckvn, w_v_ref[...])
    one_lane = lax.broadcasted_iota(jnp.int32, (1, HEAD_PAD), 1) == MLA_V
    kTn = _dot_t(w_kT_ref[...], ckvn)
    krT = _dot_t(w_krT_ref[...], hb)
    kr = (krT[:MLA_ROPE] * ckT_ref[...] + krT[MLA_ROPE:] * skT_ref[...]).astype(BF16)
    zpad = jnp.zeros((HEAD_PAD - MLA_NOPE - MLA_ROPE, kr.shape[1]), BF16)
    for hd in range(MLA_HEADS):
        q_ref[0, hd] = q[:, hd * HEAD_PAD:(hd + 1) * HEAD_PAD].astype(BF16)
        vh = vf[:, hd * HEAD_PAD:(hd + 1) * HEAD_PAD]
        v_ref[0, hd] = jnp.where(one_lane, 1.0, vh).astype(BF16)
        kT_ref[0, hd, 0, 0:MLA_NOPE, :] = kTn[hd * MLA_NOPE:(hd + 1) * MLA_NOPE].astype(BF16)
        kT_ref[0, hd, 0, MLA_NOPE:MLA_NOPE + MLA_ROPE, :] = kr
        kT_ref[0, hd, 0, MLA_NOPE + MLA_ROPE:, :] = zpad


def pre_mix(x, mods, norm_g, wts, tabs):
    bsz, seq, d = x.shape
    nt = seq // TM
    const = lambda *shape: pl.BlockSpec(shape, lambda i, b: (0,) * len(shape))
    tile = lambda w: pl.BlockSpec((1, TM, w), lambda i, b: (b, i, 0))
    nq = MLA_HEADS * HEAD_PAD
    in_specs = [
        tile(d),
        pl.BlockSpec((1, N_MOD, d), lambda i, b: (jnp.where(i == nt - 1, bsz, b), 0, 0)),
        const(1, d),
        const(d, P_TOTAL),
        const(1, MLA_Q_RANK),
        const(1, MLA_KV_RANK),
        const(MLA_Q_RANK, 2 * nq),
        const(MLA_HEADS * MLA_NOPE, MLA_KV_RANK),
        const(MLA_KV_RANK, nq),
        const(2 * MLA_ROPE, d),
        pl.BlockSpec((TM, nq), lambda i, b: (i, 0)),
        pl.BlockSpec((TM, nq), lambda i, b: (i, 0)),
        pl.BlockSpec((MLA_ROPE, TM), lambda i, b: (0, i)),
        pl.BlockSpec((MLA_ROPE, TM), lambda i, b: (0, i)),
    ]
    head_rows = pl.BlockSpec((1, MLA_HEADS, TM, HEAD_PAD), lambda i, b: (b, 0, i, 0))
    out_specs = [
        head_rows,
        pl.BlockSpec((1, MLA_HEADS, 1, HEAD_PAD, TM), lambda i, b: (b, 0, i, 0, 0)),
        head_rows,
        tile(GDN_QKV), tile(GDN_HEADS * GDN_DV),
        tile(GLA_HEADS * GLA_DK), tile(GLA_HEADS * GLA_DK),
        tile(GLA_HEADS * GLA_DV), tile(GLA_HEADS * GLA_DV),
        tile(LANE),
    ]
    sd = jax.ShapeDtypeStruct
    out_shape = [
        sd((bsz, MLA_HEADS, seq, HEAD_PAD), BF16),
        sd((bsz, MLA_HEADS, nt, HEAD_PAD, TM), BF16),
        sd((bsz, MLA_HEADS, seq, HEAD_PAD), BF16),
        sd((bsz, seq, GDN_QKV), F32), sd((bsz, seq, GDN_HEADS * GDN_DV), F32),
        sd((bsz, seq, GLA_HEADS * GLA_DK), F32), sd((bsz, seq, GLA_HEADS * GLA_DK), F32),
        sd((bsz, seq, GLA_HEADS * GLA_DV), F32), sd((bsz, seq, GLA_HEADS * GLA_DV), F32),
        sd((bsz, seq, LANE), F32),
    ]
    return pl.pallas_call(
        _pre_mix_kernel, out_shape=out_shape, grid=(nt, bsz), in_specs=in_specs, out_specs=out_specs,
        compiler_params=_cparams(("arbitrary", "arbitrary")), name="pre_mix",
    )(x, mods, norm_g, wts["w_in"], wts["q_norm"], wts["kv_norm"], wts["w_uq"], wts["w_kT"], wts["w_v"],
      wts["w_krT"], tabs["cq"], tabs["sq"], tabs["ckT"], tabs["skT"])


def _flash_kernel(q_ref, kT_ref, v_ref, o_ref, *, n_groups, kv_group):
    ctx = kT_ref.shape[2] - 1
    qs = [q_ref[0, hh] for hh in range(2)]

    def attend(carry, chunks):
        kts = [jnp.concatenate([kT_ref[0, hh, c] for c in chunks], axis=1) for hh in range(2)]
        vs = [jnp.concatenate([v_ref[0, hh, c * TM:(c + 1) * TM, :] for c in chunks], axis=0) for hh in range(2)]
        ss = [_dot(qs[hh], kts[hh]) for hh in range(2)]
        out = []
        for hh in range(2):
            m_new = jnp.max(ss[hh], axis=-1, keepdims=True)
            if carry is not None:
                m, acc = carry[2 * hh], carry[2 * hh + 1]
                m_new = jnp.maximum(m, m_new)
            p = jnp.exp2(ss[hh] - m_new).astype(BF16)
            pv = _dot(p, vs[hh])
            out += [m_new, pv if carry is None else jnp.exp2(m - m_new) * acc + pv]
        return tuple(out)

    carry = attend(None, [ctx] + list(range(kv_group)))
    for j in range(1, n_groups):
        carry = attend(carry, list(range(j * kv_group, (j + 1) * kv_group)))
    o_ref[0] = jnp.concatenate([carry[2 * hh + 1][:, :MLA_V] / carry[2 * hh + 1][:, MLA_V:MLA_V + 1]
                                for hh in range(2)], axis=-1)


def mla_attention(q, kT, v):
    bsz, nh, seq, _ = q.shape
    nt = seq // TM
    t_latent = seq - CTX_LEN
    kv_group = KV_GROUP if (nt - 1) % KV_GROUP == 0 else 2
    assert t_latent % TQ == 0 and (nt - 1) % kv_group == 0
    kv_specs = [pl.BlockSpec((1, 2, nt, HEAD_PAD, TM), lambda b, h, i: (b, h, 0, 0, 0)),
                pl.BlockSpec((1, 2, seq, HEAD_PAD), lambda b, h, i: (b, h, 0, 0))]
    sem = ("arbitrary", "arbitrary", "arbitrary")
    o_latent = pl.pallas_call(
        functools.partial(_flash_kernel, n_groups=(nt - 1) // kv_group, kv_group=kv_group),
        out_shape=jax.ShapeDtypeStruct((bsz, t_latent, nh * MLA_V), F32),
        grid=(bsz, nh // 2, t_latent // TQ),
        in_specs=[pl.BlockSpec((1, 2, TQ, HEAD_PAD), lambda b, h, i: (b, h, i, 0))] + kv_specs,
        out_specs=pl.BlockSpec((1, TQ, 2 * MLA_V), lambda b, h, i: (b, i, h)),
        compiler_params=_cparams(sem), name="mla_attention_latent",
    )(q, kT, v)
    o_context = pl.pallas_call(
        functools.partial(_flash_kernel, n_groups=0, kv_group=0),
        out_shape=jax.ShapeDtypeStruct((bsz, CTX_LEN, nh * MLA_V), F32),
        grid=(bsz, nh // 2, 1),
        in_specs=[pl.BlockSpec((1, 2, TM, HEAD_PAD), lambda b, h, i: (b, h, nt - 1, 0)),
                  pl.BlockSpec((1, 2, 1, HEAD_PAD, TM), lambda b, h, i: (b, h, nt - 1, 0, 0)),
                  pl.BlockSpec((1, 2, TM, HEAD_PAD), lambda b, h, i: (b, h, nt - 1, 0))],
        out_specs=pl.BlockSpec((1, TM, 2 * MLA_V), lambda b, h, i: (b, 0, h)),
        compiler_params=_cparams(sem), name="mla_attention_context",
    )(q, kT, v)
    return o_latent, o_context


def _split3(x):
    h1 = x.astype(BF16)
    r1 = x - h1.astype(F32)
    h2 = r1.astype(BF16)
    h3 = (r1 - h2.astype(F32)).astype(BF16)
    return h1, h2, h3


def _dot_ta(a, b):
    return lax.dot_general(a, b, (((0,), (0,)), ((), ())), preferred_element_type=F32)


def _tri(n, d):
    r = lax.broadcasted_iota(jnp.int32, (n, n), 0)
    c = lax.broadcasted_iota(jnp.int32, (n, n), 1)
    return jnp.where(c <= r if d == 0 else c >= r, 1.0, 0.0).astype(BF16)


def _lane_head(width, per_head):
    return lax.broadcasted_iota(jnp.int32, (1, width), 1) // per_head


def _stack_heads(x, n_heads, per_head):
    lh = _lane_head(x.shape[1], per_head) % n_heads
    return jnp.concatenate([jnp.where(lh == h, x, 0.0) for h in range(n_heads)], axis=0)


def _chunk_maps(n_chunks, n_latent):
    fwd = lambda n: (n + n_latent) % n_chunks
    bwd = lambda n: n_chunks - 1 - n
    return fwd, bwd


def _job_specs(widths, n_chunks, n_latent, seq, nb):
    per = RC // HALO
    last = seq // HALO - 1
    specs = []
    for cmap in _chunk_maps(n_chunks, n_latent):
        chunk = lambda w, cmap=cmap: pl.BlockSpec((nb, RC, w), lambda b, n: (b, cmap(n), 0))
        if widths[0] is not None:
            w0 = widths[0]
            specs += [chunk(w0),
                      pl.BlockSpec((nb, HALO, w0), lambda b, n, cmap=cmap: (b, jnp.maximum(cmap(n) * per - 1, 0), 0)),
                      pl.BlockSpec((nb, HALO, w0), lambda b, n, cmap=cmap: (b, jnp.minimum((cmap(n) + 1) * per, last), 0))]
        specs += [chunk(w) for w in widths[1:]]
    return specs


def _lockstep(jobs):
    live = [(job, next(job)) for job in jobs]
    while live:
        groups = {}
        for ji, (_, reqs) in enumerate(live):
            for ri, (fn, a, b) in enumerate(reqs):
                groups.setdefault((fn, id(b)), []).append((ji, ri, a, b))
        results = [[None] * len(reqs) for _, reqs in live]
        for (fn, _), members in groups.items():
            if fn in (_dot, _dot_t) and len(members) > 1:
                out = fn(jnp.concatenate([a for _, _, a, _ in members], axis=0), members[0][3])
                row = 0
                for ji, ri, a, _ in members:
                    results[ji][ri] = out[row:row + a.shape[0]]
                    row += a.shape[0]
            else:
                for ji, ri, a, b in members:
                    results[ji][ri] = fn(a, b)
        nxt = []
        for (job, _), res in zip(live, results):
            try:
                nxt.append((job, job.send(res)))
            except StopIteration:
                pass
        live = nxt


def _gdn_job(d, c, n_chunks, n_latent, qkv, prev, nxt, misc, conv_w_ref, esel,
             alog_ref, dtb_ref, gsum, mask_ref, s_ref, s_idx, store_o):
    hw = GDN_HEADS * GDN_DK
    prev_ok = (c != 0) & (c != n_latent)
    next_ok = (c != n_latent - 1) & (c != n_chunks - 1)
    x = _silu(_conv3(jnp.concatenate([prev, qkv, nxt], axis=0), conv_w_ref, slice(None), prev_ok, next_ok))
    q, k, v = x[:, :hw], x[:, hw:2 * hw], x[:, 2 * hw:]
    sq = [q * q, k * k]
    hi = [a.astype(BF16) for a in sq]
    lo = [(a - h.astype(F32)).astype(BF16) for a, h in zip(sq, hi)]
    r = yield [(_dot, a, gsum) for a in hi + lo]
    q = q * lax.rsqrt(r[0] + r[2] + NORM_EPS) * GDN_DK ** -0.5
    k = k * lax.rsqrt(r[1] + r[3] + NORM_EPS)

    lane = lax.broadcasted_iota(jnp.int32, (1, LANE), 1)
    gates = jnp.where((lane >= M_A) & (lane < M_B),
                      -jnp.exp(alog_ref[...]) * _softplus(misc + dtb_ref[...]), jax.nn.sigmoid(misc))
    r = yield [(_dot, p, esel[d]) for p in _split3(gates)]
    gb = r[0] + r[1] + r[2]
    g, beta = gb[:, :hw], gb[:, hw:]
    tri = _tri(RC, d)
    r = yield [(_dot, tri, p) for p in _split3(g)]
    gc = r[0] + r[1] + r[2]
    total = gc[RC - 1:RC] if d == 0 else gc[0:1]
    eg = jnp.exp(gc)
    kb = k * beta
    rhs = jnp.concatenate([v * beta, kb * eg], axis=1)
    qd = q * eg
    kend = k * jnp.exp(total - gc)

    bd = lambda m: _stack_heads(m, GDN_HEADS, RC)
    ks = bd(k).astype(BF16)
    kk, qk = yield [(_dot_t, kb.astype(BF16), ks), (_dot_t, q.astype(BF16), ks)]
    eye, strict = mask_ref[0], mask_ref[1 + d]
    gc_j = jnp.sum(eye * gc, axis=0, keepdims=True)
    decay = jnp.exp(jnp.minimum(gc - gc_j, 0.0))
    lmat = strict * (kk * decay)
    amat = (strict + eye) * (qk * decay)

    n_sq = int(math.log2(RC)) - 1
    p = lmat
    t = eye - lmat
    (p,) = yield [(_dot, p, bd(p))]
    for _ in range(n_sq - 1):
        (r,) = yield [(_dot, jnp.concatenate([p, t], axis=0), bd(p))]
        p, t = r[:RC], t + r[RC:]
    (r,) = yield [(_dot, t, bd(p))]
    t = t + r
    (uw,) = yield [(_dot, t.astype(BF16), bd(rhs).astype(BF16))]
    u, w = uw[:, :hw], uw[:, hw:]

    s = s_ref[s_idx]
    (ws,) = yield [(_dot, jnp.concatenate([w, qd], axis=0).astype(BF16), s.astype(BF16))]
    v_new = u - ws[:RC]
    o_intra, kv = yield [(_dot, amat.astype(BF16), bd(v_new).astype(BF16)), (_dot_ta, kend, v_new)]
    store_o(ws[RC:] + o_intra)
    rh = lax.broadcasted_iota(jnp.int32, (hw, hw), 0) // GDN_DK
    ch = lax.broadcasted_iota(jnp.int32, (hw, hw), 1) // GDN_DV
    s_ref[s_idx] = s * jnp.exp(total) + jnp.where(rh == ch, kv, 0.0)


def _store_at(ref, bi):
    def store(val):
        ref[bi] = val
    return store


_GLA_LEVELS = int(math.log2(RC))


def _gdn_masks():
    i = np.arange(RC)[:, None]
    j = (np.arange(GDN_HEADS * RC) % RC)[None, :]
    return np.stack([i == j, j < i, j > i]).astype(np.float32)


def _gla_pair_masks():
    i = np.arange(RC)[:, None]
    j = (np.arange(GLA_HEADS * RC) % RC)[None, :]
    masks = np.zeros((2, _GLA_LEVELS + 1, RC, GLA_HEADS * RC), np.float32)
    for d in range(2):
        masks[d, 0] = i == j
        for ls in range(_GLA_LEVELS):
            s = 1 << ls
            same = (i >> (ls + 1)) == (j >> (ls + 1))
            i_late, j_late = (i & (2 * s - 1)) >= s, (j & (2 * s - 1)) >= s
            masks[d, 1 + ls] = same & ((i_late & ~j_late) if d == 0 else (~i_late & j_late))
    return masks


def _block_mid_rows(b, ls, d):
    n, width = b.shape
    s = 1 << ls
    off = s - 1 if d == 0 else s
    if 2 * s >= SUBLANE:
        return jnp.concatenate([jnp.broadcast_to(b[m + off:m + off + 1], (2 * s, width))
                                for m in range(0, n, 2 * s)], axis=0)
    pos = lax.broadcasted_iota(jnp.int32, (n, 1), 0) & (2 * s - 1)
    out = None
    for r in range(2 * s):
        shift = (r - off) % n
        rolled = b if shift == 0 else pltpu.roll(b, shift, axis=0)
        out = rolled if out is None else jnp.where(pos == r, rolled, out)
    return out


def _gla_job(d, q, k, v, misc, wgk, bgk_ref, pair_ref, st_ref, s_idx, store_o):
    q = q * GLA_DK ** -0.5
    (gk,) = yield [(_dot, misc.astype(BF16), wgk[d])]
    la = -_softplus(-(gk + bgk_ref[d])) * (1.0 / GLA_NORMALIZER)
    tri = _tri(RC, d)
    r = yield [(_dot, tri, p) for p in _split3(la)]
    b = r[0] + r[1] + r[2]
    blast = b[RC - 1:RC] if d == 0 else b[0:1]

    lhs, rhs = [q.astype(BF16)], [_stack_heads(k, GLA_HEADS, GLA_DK).astype(BF16)]
    for ls in range(_GLA_LEVELS):
        ref = _block_mid_rows(b, ls, d)
        lhs.append((q * jnp.exp(jnp.minimum(b - ref, 0.0))).astype(BF16))
        rhs.append(_stack_heads(k * jnp.exp(jnp.minimum(ref - b, 0.0)), GLA_HEADS, GLA_DK).astype(BF16))
    r = yield [(_dot_t, a, bb) for a, bb in zip(lhs, rhs)]
    amat = pair_ref[d, 0] * r[0]
    for lvl in range(1, _GLA_LEVELS + 1):
        amat = amat + pair_ref[d, lvl] * r[lvl]

    st = st_ref[s_idx]
    kend = k * jnp.exp(blast - b)
    o_intra, o_inter, vk = yield [
        (_dot, amat.astype(BF16), _stack_heads(v, GLA_HEADS, GLA_DV).astype(BF16)),
        (_dot_t, (q * jnp.exp(b)).astype(BF16), st.astype(BF16)),
        (_dot_ta, v, kend)]
    store_o(o_intra + o_inter)
    rh = lax.broadcasted_iota(jnp.int32, st.shape, 0) // GLA_DV
    ch = lax.broadcasted_iota(jnp.int32, st.shape, 1) // GLA_DK
    st_ref[s_idx] = st * jnp.exp(blast) + jnp.where(rh == ch, vk, 0.0)


def _mixers_kernel(*refs, n_chunks, n_latent):
    (qkv_f, prev_f, next_f, misc_f, qkv_r, prev_r, next_r, misc_r,
     lq_f, lk_f, lv_f, lq_r, lk_r, lv_r,
     conv_w, esel, alog, dtb, gsum, gdn_mask, wgk, bgk, gla_pairs,
     gdn_f, gdn_r, gla_f, gla_r, s_ref, st_ref) = refs
    n = pl.program_id(1)

    @pl.when(n == 0)
    def _():
        s_ref[...] = jnp.zeros_like(s_ref)
        st_ref[...] = jnp.zeros_like(st_ref)

    fwd, bwd = _chunk_maps(n_chunks, n_latent)
    esel, wgk = [esel[0], esel[1]], [wgk[0], wgk[1]]
    gdn_consts = (conv_w, esel, alog, dtb, gsum[...], gdn_mask, s_ref)
    jobs = []
    for bi in range(qkv_f.shape[0]):
        jobs.append(_gdn_job(0, fwd(n), n_chunks, n_latent, qkv_f[bi], prev_f[bi], next_f[bi], misc_f[bi],
                             *gdn_consts, 2 * bi, _store_at(gdn_f, bi)))
        jobs.append(_gdn_job(1, bwd(n), n_chunks, n_latent, qkv_r[bi], prev_r[bi], next_r[bi], misc_r[bi],
                             *gdn_consts, 2 * bi + 1, _store_at(gdn_r, bi)))
    for bi in range(lq_f.shape[0]):
        jobs.append(_gla_job(0, lq_f[bi], lk_f[bi], lv_f[bi], misc_f[bi], wgk, bgk, gla_pairs, st_ref, 2 * bi,
                             _store_at(gla_f, bi)))
        jobs.append(_gla_job(1, lq_r[bi], lk_r[bi], lv_r[bi], misc_r[bi], wgk, bgk, gla_pairs, st_ref, 2 * bi + 1,
                             _store_at(gla_r, bi)))
    _lockstep(jobs)


def recurrent_mixers(gqkv, misc, lq, lk, lv, wts):
    bsz, seq, hk = lq.shape
    hv = lv.shape[2]
    hw = GDN_HEADS * GDN_DV
    n_chunks = seq // RC
    n_latent = (seq - CTX_LEN) // RC
    nb = RNB if bsz % RNB == 0 else 1
    const = lambda *shape: pl.BlockSpec(shape, lambda b, n: (0,) * len(shape))
    fwd, bwd = _chunk_maps(n_chunks, n_latent)
    out = lambda w: jax.ShapeDtypeStruct((bsz, seq, w), F32)
    out_spec = lambda w, cmap: pl.BlockSpec((nb, RC, w), lambda b, n: (b, cmap(n), 0))
    return pl.pallas_call(
        functools.partial(_mixers_kernel, n_chunks=n_chunks, n_latent=n_latent),
        out_shape=[out(hw), out(hw), out(hv), out(hv)],
        grid=(bsz // nb, n_chunks),
        in_specs=_job_specs((GDN_QKV, LANE), n_chunks, n_latent, seq, nb)
        + _job_specs((None, hk, hk, hv), n_chunks, n_latent, seq, nb)
        + [const(3, GDN_QKV), const(2, LANE, 2 * hw), const(1, LANE), const(1, LANE), const(hw, hw),
           const(3, RC, hw), const(2, LANE, hk), const(2, 1, hk), const(2, _GLA_LEVELS + 1, RC, GLA_HEADS * RC)],
        out_specs=[out_spec(hw, fwd), out_spec(hw, bwd), out_spec(hv, fwd), out_spec(hv, bwd)],
        scratch_shapes=[pltpu.VMEM((2 * nb, hw, hw), F32), pltpu.VMEM((2 * nb, hv, hk), F32)],
        compiler_params=_cparams(("arbitrary", "arbitrary")),
        name="recurrent_mixers",
    )(gqkv, gqkv, gqkv, misc, gqkv, gqkv, gqkv, misc, lq, lk, lv, lq, lk, lv,
      wts["gdn_conv_w"], wts["gdn_esel"], wts["gdn_alog"], wts["gdn_dtb"], wts["gsum"], jnp.asarray(_gdn_masks()),
      wts["gla_wgk"], wts["gla_bgk"], jnp.asarray(_gla_pair_masks()))


def _group_mean_sq(x, gmat):
    xsq = x * x
    hi = xsq.astype(BF16)
    lo = (xsq - hi.astype(F32)).astype(BF16)
    return _dot(hi, gmat) + _dot(lo, gmat)


def _post_mix_kernel(x_ref, mod_ref, mla_l_ref, mla_c_ref, gdn_f_ref, gdn_r_ref, gz_ref, gla_f_ref, gla_r_ref,
                     lg_ref, gmat_ref, gdn_g_ref, gla_g_ref, w_out_ref, post_g_ref, o_ref):
    is_ctx = pl.program_id(1) == pl.num_programs(1) - 1
    mla = jnp.where(is_ctx, mla_c_ref[0], mla_l_ref[0])
    gmat = gmat_ref[...]
    gdn = gdn_f_ref[0] + gdn_r_ref[0]
    gdn = gdn * lax.rsqrt(_group_mean_sq(gdn, gmat) + NORM_EPS) * gdn_g_ref[...] * _silu(gz_ref[0])
    gla = gla_f_ref[0] + gla_r_ref[0]
    gla = gla * lax.rsqrt(_group_mean_sq(gla, gmat) + NORM_EPS) * gla_g_ref[...] * _silu(lg_ref[0])
    merged = jnp.concatenate([mla, gdn, gla], axis=-1).astype(BF16)
    y = _dot(merged, w_out_ref[...])
    gate = mod_ref[0, 2:3, :]
    o_ref[0] = x_ref[0] + gate * _rms(y, post_g_ref[...])


def post_mix(x, mods, mla_l, mla_c, gdn_f, gdn_r, gz, gla_f, gla_r, lg, wts, post_g):
    bsz, seq, d = x.shape
    nt = seq // TM
    const = lambda *shape: pl.BlockSpec(shape, lambda b, i: (0,) * len(shape))
    tile = lambda w: pl.BlockSpec((1, TM, w), lambda b, i: (b, i, 0))
    hw = GDN_HEADS * GDN_DV
    hm = MLA_HEADS * MLA_V
    return pl.pallas_call(
        _post_mix_kernel,
        out_shape=jax.ShapeDtypeStruct(x.shape, F32),
        grid=(bsz, nt),
        in_specs=[tile(d),
                  pl.BlockSpec((1, N_MOD, d), lambda b, i: (jnp.where(i == nt - 1, bsz, b), 0, 0)),
                  pl.BlockSpec((1, TM, hm), lambda b, i: (b, jnp.minimum(i, nt - 2), 0)),
                  pl.BlockSpec((1, TM, hm), lambda b, i: (b, 0, 0)),
                  tile(hw), tile(hw), tile(hw), tile(hw), tile(hw), tile(hw),
                  const(hw, hw), const(1, hw), const(1, hw), const(D_MIX, d), const(1, d)],
        out_specs=tile(d),
        compiler_params=_cparams(("arbitrary", "arbitrary")),
        name="post_mix",
    )(x, mods, mla_l, mla_c, gdn_f, gdn_r, gz, gla_f, gla_r, lg, wts["gmat"], wts["gdn_norm"], wts["gla_norm"],
      wts["w_out"], post_g)


def _conv3(z, w_ref, cols, prev_ok, next_ok, pad=0.0):
    rows = z.shape[0]
    n = rows - 2 * HALO
    z = jnp.concatenate([jnp.where(prev_ok, z[:HALO], pad), z[HALO:HALO + n],
                         jnp.where(next_ok, z[HALO + n:], pad)], axis=0)
    zm = pltpu.roll(z, 1, axis=0)[HALO:HALO + n]
    zp = pltpu.roll(z, rows - 1, axis=0)[HALO:HALO + n]
    zc = z[HALO:HALO + n]
    return zm * w_ref[0:1, cols] + zc * w_ref[1:2, cols] + zp * w_ref[2:3, cols]


def _ffn_kernel(x_ref, xp_ref, xn_ref, mod_ref, pre_g_ref, w_in_ref, b_in_ref, cw_ref, cb_ref,
                w_out_ref, post_g_ref, o_ref, *, nt):
    i = pl.program_id(1)
    prev_ok = (i > 0) & (i < nt - 1)
    next_ok = i < nt - 2
    x = x_ref[0]
    xx = jnp.concatenate([xp_ref[0], x, xn_ref[0]], axis=0)
    shift = mod_ref[0, 3:4, :]
    scale = mod_ref[0, 4:5, :]
    hb = (_rms(xx, pre_g_ref[...]) * (1.0 + scale) + shift).astype(BF16)
    n_chunks = FFN_HIDDEN // FFN_CHUNK
    cols = [(slice(j * FFN_CHUNK, (j + 1) * FFN_CHUNK),
             slice(FFN_HIDDEN + j * FFN_CHUNK, FFN_HIDDEN + (j + 1) * FFN_CHUNK)) for j in range(n_chunks)]
    acts = []
    for ca, cg in cols:
        za, zg = _dot(hb, w_in_ref[:, ca]), _dot(hb, w_in_ref[:, cg])
        bias = lambda c: cb_ref[:, c] + b_in_ref[:, c] * (cw_ref[0:1, c] + cw_ref[1:2, c] + cw_ref[2:3, c])
        a = _conv3(za, cw_ref, ca, prev_ok, next_ok, -b_in_ref[:, ca]) + bias(ca)
        g = _conv3(zg, cw_ref, cg, prev_ok, next_ok, -b_in_ref[:, cg]) + bias(cg)
        acts.append((a * _silu(g)).astype(BF16))
    acc = _dot(jnp.concatenate(acts, axis=1), w_out_ref[...])
    gate = mod_ref[0, 5:6, :]
    o_ref[0] = x + gate * _rms(acc, post_g_ref[...])


def conv_ffn(x, mods, pre_g, wts, post_g, keep_context=True):
    bsz, seq, d = x.shape
    nt = seq // TM
    n_out = nt if keep_context else nt - 1
    nb = seq // HALO
    per = TM // HALO
    const = lambda *shape: pl.BlockSpec(shape, lambda b, i: (0,) * len(shape))
    tile = pl.BlockSpec((1, TM, d), lambda b, i: (b, i, 0))
    return pl.pallas_call(
        functools.partial(_ffn_kernel, nt=nt),
        out_shape=jax.ShapeDtypeStruct((bsz, n_out * TM, d), F32),
        grid=(bsz, n_out),
        in_specs=[tile,
                  pl.BlockSpec((1, HALO, d), lambda b, i: (b, jnp.maximum(i * per - 1, 0), 0)),
                  pl.BlockSpec((1, HALO, d), lambda b, i: (b, jnp.minimum((i + 1) * per, nb - 1), 0)),
                  pl.BlockSpec((1, N_MOD, d), lambda b, i: (jnp.where(i == nt - 1, bsz, b), 0, 0)),
                  const(1, d), const(d, 2 * FFN_HIDDEN), const(1, 2 * FFN_HIDDEN),
                  const(3, 2 * FFN_HIDDEN), const(1, 2 * FFN_HIDDEN), const(FFN_HIDDEN, d), const(1, d)],
        out_specs=tile,
        compiler_params=_cparams(("arbitrary", "arbitrary")),
        name="conv_ffn",
    )(x, x, x, mods, pre_g, wts["ffn_w_in"], wts["ffn_b_in"], wts["ffn_conv_w"], wts["ffn_conv_b"],
      wts["ffn_w_out"], post_g)


def _rope_tables(t_latent):
    n = MLA_ROPE // 4
    t = jnp.arange(t_latent)
    row = (t // GRID_W).astype(F32)
    col = (t % GRID_W).astype(F32)
    inv = ROPE_THETA ** (-jnp.arange(n, dtype=F32) / n)
    ang = jnp.stack([row[:, None] * inv, col[:, None] * inv], axis=1)
    cos, sin = jnp.cos(ang), jnp.sin(ang)
    c32 = jnp.stack([cos, cos], axis=2).reshape(t_latent, MLA_ROPE)
    s32 = jnp.stack([-sin, sin], axis=2).reshape(t_latent, MLA_ROPE)
    c32 = jnp.concatenate([c32, jnp.ones((CTX_LEN, MLA_ROPE), F32)], axis=0)
    s32 = jnp.concatenate([s32, jnp.zeros((CTX_LEN, MLA_ROPE), F32)], axis=0)
    seq = CTX_LEN + t_latent
    qscale = (MLA_NOPE + MLA_ROPE) ** -0.5 * math.log2(math.e)
    pad = HEAD_PAD - MLA_NOPE - MLA_ROPE
    cq = jnp.concatenate([jnp.ones((seq, MLA_NOPE), F32), c32, jnp.zeros((seq, pad), F32)], axis=1) * qscale
    sq = jnp.concatenate([jnp.zeros((seq, MLA_NOPE), F32), s32, jnp.zeros((seq, pad), F32)], axis=1) * qscale
    return {"cq": jnp.tile(cq, (1, MLA_HEADS)), "sq": jnp.tile(sq, (1, MLA_HEADS)),
            "ckT": c32.T, "skT": s32.T}


_ROPE_PARTNER = np.arange(MLA_ROPE) ^ (MLA_ROPE // 4)


def _prep_layer(i, w_in, mla_q_norm, mla_w_uq, mla_kv_norm, mla_w_ukv, gdn_conv_w, gdn_a_log, gdn_dt_bias,
                gdn_norm, gla_w_gk, gla_b_gk, gla_norm, w_out, ffn_w_in, ffn_b_in, ffn_conv_w, ffn_conv_b,
                ffn_w_out):
    o = IN_OFFS
    w = w_in[i]
    d = w.shape[0]
    piece = lambda k: w[:, o[k]:o[k + 1]]
    misc_pad = LANE - (M_LR + 2 * GLA_GATE_RANK)
    w_in_p = jnp.concatenate(
        [piece(0), piece(1), piece(3), piece(4), piece(7), piece(8), piece(9), piece(10),
         piece(2), piece(5), piece(6), piece(11), jnp.zeros((d, misc_pad), F32)], axis=1).astype(BF16)
    kr = piece(2)
    w_krT = jnp.concatenate([kr, kr[:, _ROPE_PARTNER]], axis=1).T.astype(BF16)

    hq = MLA_NOPE + MLA_ROPE
    uq = mla_w_uq[i].reshape(MLA_Q_RANK, MLA_HEADS, hq)
    zq = lambda n: jnp.zeros((MLA_Q_RANK, MLA_HEADS, n), F32)
    plain = jnp.concatenate([uq, zq(HEAD_PAD - hq)], axis=2)
    partner = jnp.concatenate([zq(MLA_NOPE), uq[:, :, MLA_NOPE:][:, :, _ROPE_PARTNER], zq(HEAD_PAD - hq)], axis=2)
    w_uq = jnp.concatenate([plain.reshape(MLA_Q_RANK, -1), partner.reshape(MLA_Q_RANK, -1)], axis=1).astype(BF16)

    ukv = mla_w_ukv[i].reshape(MLA_KV_RANK, MLA_HEADS, MLA_NOPE + MLA_V)
    w_kT = ukv[:, :, :MLA_NOPE].reshape(MLA_KV_RANK, -1).T.astype(BF16)
    w_v = jnp.concatenate([ukv[:, :, MLA_NOPE:], jnp.zeros((MLA_KV_RANK, MLA_HEADS, HEAD_PAD - MLA_V), F32)],
                          axis=2).reshape(MLA_KV_RANK, -1).astype(BF16)

    hw = GDN_HEADS * GDN_DV
    lane_head = np.arange(hw) // GDN_DV
    same_head = (lane_head[:, None] == lane_head[None, :]).astype(np.float32)
    esel = np.zeros((2, LANE, 2 * hw), np.float32)
    for dd in range(2):
        for h in range(GDN_HEADS):
            esel[dd, M_A + dd * GDN_HEADS + h, h * GDN_DV:(h + 1) * GDN_DV] = 1.0
            esel[dd, M_B + dd * GDN_HEADS + h, hw + h * GDN_DV:hw + (h + 1) * GDN_DV] = 1.0
    wgk = jnp.zeros((2, LANE, GLA_HEADS * GLA_DK), F32)
    for dd in range(2):
        r0 = M_LR + dd * GLA_GATE_RANK
        wgk = wgk.at[dd, r0:r0 + GLA_GATE_RANK].set(gla_w_gk[i, dd])
    return {
        "w_in": w_in_p, "w_krT": w_krT, "w_uq": w_uq, "w_kT": w_kT, "w_v": w_v,
        "q_norm": mla_q_norm[i][None], "kv_norm": mla_kv_norm[i][None],
        "gdn_conv_w": gdn_conv_w[i], "gdn_esel": jnp.asarray(esel).astype(BF16),
        "gdn_alog": jnp.zeros((1, LANE), F32).at[0, M_A:M_B].set(gdn_a_log[i].reshape(-1)),
        "gdn_dtb": jnp.zeros((1, LANE), F32).at[0, M_A:M_B].set(gdn_dt_bias[i].reshape(-1)),
        "gsum": jnp.asarray(same_head).astype(BF16),
        "gla_wgk": wgk.astype(BF16), "gla_bgk": gla_b_gk[i][:, None, :],
        "gmat": jnp.asarray(same_head / GDN_DV).astype(BF16), "gdn_norm": jnp.tile(gdn_norm[i], GDN_HEADS)[None],
        "gla_norm": jnp.tile(gla_norm[i], GLA_HEADS)[None], "w_out": w_out[i].astype(BF16),
        "ffn_w_in": ffn_w_in[i].astype(BF16), "ffn_b_in": ffn_b_in[i][None], "ffn_conv_w": ffn_conv_w[i],
        "ffn_conv_b": ffn_conv_b[i][None], "ffn_w_out": ffn_w_out[i].astype(BF16),
    }


def kernel(x, c, ctx, c_ctx, w_ada, b_ada, norm_mix_pre, norm_mix_post, norm_ffn_pre, norm_ffn_post,
           w_in, mla_q_norm, mla_w_uq, mla_kv_norm, mla_w_ukv, gdn_conv_w, gdn_a_log, gdn_dt_bias,
           gdn_norm, gla_w_gk, gla_b_gk, gla_norm, w_out, ffn_w_in, ffn_b_in, ffn_conv_w, ffn_conv_b,
           ffn_w_out):
    bsz, t_latent, d = x.shape
    assert ctx.shape[1] == CTX_LEN and t_latent % TM == 0 and t_latent % GRID_W == 0
    depth = w_ada.shape[0]
    xs = jnp.concatenate([x, ctx], axis=1)
    mod_rows = -(-(bsz + 1) // SUBLANE) * SUBLANE
    cvec = jnp.concatenate([c, c_ctx[None], jnp.zeros((mod_rows - bsz - 1, d), F32)], axis=0)
    mods_all = ada_modulation(cvec, w_ada, b_ada).reshape(depth, mod_rows, N_MOD, d)
    tabs = _rope_tables(t_latent)
    for i in range(depth):
        mods = mods_all[i]
        wts = _prep_layer(i, w_in, mla_q_norm, mla_w_uq, mla_kv_norm, mla_w_ukv, gdn_conv_w, gdn_a_log,
                          gdn_dt_bias, gdn_norm, gla_w_gk, gla_b_gk, gla_norm, w_out, ffn_w_in, ffn_b_in,
                          ffn_conv_w, ffn_conv_b, ffn_w_out)
        q, kT, v, gqkv, gz, lq, lk, lv, lg, misc = pre_mix(xs, mods, norm_mix_pre[i][None], wts, tabs)
        mla_l, mla_c = mla_attention(q, kT, v)
        gdn_f, gdn_r, gla_f, gla_r = recurrent_mixers(gqkv, misc, lq, lk, lv, wts)
        xs = post_mix(xs, mods, mla_l, mla_c, gdn_f, gdn_r, gz, gla_f, gla_r, lg, wts, norm_mix_post[i][None])
        xs = conv_ffn(xs, mods, norm_ffn_pre[i][None], wts, norm_ffn_post[i][None], keep_context=i < depth - 1)
    return xs
```

```python
import functools
import math

import numpy as np
import jax
import jax.numpy as jnp
from jax import lax
from jax.experimental import pallas as pl
from jax.experimental.pallas import tpu as pltpu

F32 = jnp.float32
BF16 = jnp.bfloat16

GRID_W = 64
CTX_LEN = 256
N_MOD = 6
NORM_EPS = 1e-6

MLA_HEADS = 8
MLA_Q_RANK = 384
MLA_KV_RANK = 256
MLA_NOPE = 64
MLA_ROPE = 32
MLA_V = 64
ROPE_THETA = 10000.0

GDN_HEADS = 4
GDN_DK = 64
GDN_DV = 64

GLA_HEADS = 4
GLA_DK = 32
GLA_DV = 64
GLA_GATE_RANK = 16
GLA_NORMALIZER = 16.0

FFN_HIDDEN = 2560
D_MIX = MLA_HEADS * MLA_V + GDN_HEADS * GDN_DV + GLA_HEADS * GLA_DV
GDN_QKV = GDN_HEADS * (2 * GDN_DK + GDN_DV)
IN_SIZES = (MLA_Q_RANK, MLA_KV_RANK, MLA_ROPE,
            GDN_QKV, GDN_HEADS * GDN_DV, 2 * GDN_HEADS, 2 * GDN_HEADS,
            GLA_HEADS * GLA_DK, GLA_HEADS * GLA_DK, GLA_HEADS * GLA_DV, GLA_HEADS * GLA_DV,
            2 * GLA_GATE_RANK)
IN_OFFS = tuple(int(s) for s in np.cumsum((0,) + IN_SIZES))

LANE = 128
SUBLANE = 8
TM = CTX_LEN
TQ = 2 * TM
KV_GROUP = 8
HALO = SUBLANE
HEAD_PAD = LANE
FFN_CHUNK = 512
RC = 64
RNB = 8
assert RC == GDN_DK
VMEM_LIMIT = 56 * 1024 * 1024

P_CQ = 0
P_CKV = P_CQ + MLA_Q_RANK
P_GQKV = P_CKV + MLA_KV_RANK
P_GZ = P_GQKV + GDN_QKV
P_LQ = P_GZ + GDN_HEADS * GDN_DV
P_LK = P_LQ + GLA_HEADS * GLA_DK
P_LV = P_LK + GLA_HEADS * GLA_DK
P_LG = P_LV + GLA_HEADS * GLA_DV
P_MISC = P_LG + GLA_HEADS * GLA_DV
P_TOTAL = P_MISC + LANE
M_KR = 0
M_A = MLA_ROPE
M_B = M_A + 2 * GDN_HEADS
M_LR = M_B + 2 * GDN_HEADS


def _cparams(sem, vmem=VMEM_LIMIT):
    return pltpu.CompilerParams(dimension_semantics=sem, vmem_limit_bytes=vmem)


def _rms(x, g):
    return x * lax.rsqrt(jnp.mean(x * x, axis=-1, keepdims=True) + NORM_EPS) * g


def _silu(x):
    return x * jax.nn.sigmoid(x)


def _softplus(x):
    return jnp.maximum(x, 0.0) + jnp.log1p(jnp.exp(-jnp.abs(x)))


def _dot(a, b):
    return jnp.dot(a, b, preferred_element_type=F32)


def _dot_t(a, b):
    return lax.dot_general(a, b, (((1,), (1,)), ((), ())), preferred_element_type=F32)


def _ada_kernel(c_ref, w_ref, b_ref, o_ref):
    o_ref[0] = _dot(_silu(c_ref[...]), w_ref[0]) + b_ref[0]


def ada_modulation(cvec, w_ada, b_ada):
    depth, d, n = w_ada.shape
    rows = cvec.shape[0]
    tn = n // 4
    return pl.pallas_call(
        _ada_kernel,
        out_shape=jax.ShapeDtypeStruct((depth, rows, n), F32),
        grid=(depth, n // tn),
        in_specs=[pl.BlockSpec((rows, d), lambda l, j: (0, 0)),
                  pl.BlockSpec((1, d, tn), lambda l, j: (l, 0, j)),
                  pl.BlockSpec((1, 1, tn), lambda l, j: (l, 0, j))],
        out_specs=pl.BlockSpec((1, rows, tn), lambda l, j: (l, 0, j)),
        compiler_params=_cparams(("arbitrary", "arbitrary")),
        name="ada_modulation",
    )(cvec, w_ada, b_ada.reshape(depth, 1, n))


def _pre_mix_kernel(x_ref, mod_ref, g_ref, w_in_ref, qn_ref, kvn_ref, w_uq_ref, w_kT_ref, w_v_ref,
                    w_krT_ref, cq_ref, sq_ref, ckT_ref, skT_ref,
                    q_ref, kT_ref, v_ref, gqkv_ref, gz_ref, lq_ref, lk_ref, lv_ref, lg_ref, misc_ref):
    x = x_ref[0]
    shift = mod_ref[0, 0:1, :]
    scale = mod_ref[0, 1:2, :]
    h = _rms(x, g_ref[...]) * (1.0 + scale) + shift
    hb = h.astype(BF16)
    p = _dot(hb, w_in_ref[...])

    gqkv_ref[0] = p[:, P_GQKV:P_GZ]
    gz_ref[0] = p[:, P_GZ:P_LQ]
    lq_ref[0] = p[:, P_LQ:P_LK]
    lk_ref[0] = p[:, P_LK:P_LV]
    lv_ref[0] = p[:, P_LV:P_LG]
    lg_ref[0] = p[:, P_LG:P_MISC]
    misc_ref[0] = p[:, P_MISC:P_TOTAL]

    cqn = _rms(p[:, P_CQ:P_CKV], qn_ref[...]).astype(BF16)
    ckvn = _rms(p[:, P_CKV:P_GQKV], kvn_ref[...]).astype(BF16)
    nq = MLA_HEADS * HEAD_PAD
    qp = _dot(cqn, w_uq_ref[...])
    n = MLA_ROPE // 4
    lane = lax.broadcasted_iota(jnp.int32, (1, nq), 1)
    first_half = (((lane % HEAD_PAD) - MLA_NOPE) & n) == 0
    partner = jnp.where(first_half, pltpu.roll(qp, nq - n, axis=1), pltpu.roll(qp, n, axis=1))
    q = qp * cq_ref[...] + partner * sq_ref[...]
    vf = _dot(ckvn, w_v_ref[...])
    one_lane = lax.broadcasted_iota(jnp.int32, (1, HEAD_PAD), 1) == MLA_V
    kTn = _dot_t(w_kT_ref[...], ckvn)
    krT = _dot_t(w_krT_ref[...], hb)
    kr = (krT[:MLA_ROPE] * ckT_ref[...] + krT[MLA_ROPE:] * skT_ref[...]).astype(BF16)
    zpad = jnp.zeros((HEAD_PAD - MLA_NOPE - MLA_ROPE, kr.shape[1]), BF16)
    for hd in range(MLA_HEADS):
        q_ref[0, hd] = q[:, hd * HEAD_PAD:(hd + 1) * HEAD_PAD].astype(BF16)
        vh = vf[:, hd * HEAD_PAD:(hd + 1) * HEAD_PAD]
        v_ref[0, hd] = jnp.where(one_lane, 1.0, vh).astype(BF16)
        kT_ref[0, hd, 0, 0:MLA_NOPE, :] = kTn[hd * MLA_NOPE:(hd + 1) * MLA_NOPE].astype(BF16)
        kT_ref[0, hd, 0, MLA_NOPE:MLA_NOPE + MLA_ROPE, :] = kr
        kT_ref[0, hd, 0, MLA_NOPE + MLA_ROPE:, :] = zpad


def pre_mix(x, mods, norm_g, wts, tabs):
    bsz, seq, d = x.shape
    nt = seq // TM
    const = lambda *shape: pl.BlockSpec(shape, lambda i, b: (0,) * len(shape))
    tile = lambda w: pl.BlockSpec((1, TM, w), lambda i, b: (b, i, 0))
    nq = MLA_HEADS * HEAD_PAD
    in_specs = [
        tile(d),
        pl.BlockSpec((1, N_MOD, d), lambda i, b: (jnp.where(i == nt - 1, bsz, b), 0, 0)),
        const(1, d),
        const(d, P_TOTAL),
        const(1, MLA_Q_RANK),
        const(1, MLA_KV_RANK),
        const(MLA_Q_RANK, nq),
        const(MLA_HEADS * MLA_NOPE, MLA_KV_RANK),
        const(MLA_KV_RANK, nq),
        const(2 * MLA_ROPE, d),
        pl.BlockSpec((TM, nq), lambda i, b: (i, 0)),
        pl.BlockSpec((TM, nq), lambda i, b: (i, 0)),
        pl.BlockSpec((MLA_ROPE, TM), lambda i, b: (0, i)),
        pl.BlockSpec((MLA_ROPE, TM), lambda i, b: (0, i)),
    ]
    head_rows = pl.BlockSpec((1, MLA_HEADS, TM, HEAD_PAD), lambda i, b: (b, 0, i, 0))
    out_specs = [
        head_rows,
        pl.BlockSpec((1, MLA_HEADS, 1, HEAD_PAD, TM), lambda i, b: (b, 0, i, 0, 0)),
        head_rows,
        tile(GDN_QKV), tile(GDN_HEADS * GDN_DV),
        tile(GLA_HEADS * GLA_DK), tile(GLA_HEADS * GLA_DK),
        tile(GLA_HEADS * GLA_DV), tile(GLA_HEADS * GLA_DV),
        tile(LANE),
    ]
    sd = jax.ShapeDtypeStruct
    out_shape = [
        sd((bsz, MLA_HEADS, seq, HEAD_PAD), BF16),
        sd((bsz, MLA_HEADS, nt, HEAD_PAD, TM), BF16),
        sd((bsz, MLA_HEADS, seq, HEAD_PAD), BF16),
        sd((bsz, seq, GDN_QKV), F32), sd((bsz, seq, GDN_HEADS * GDN_DV), F32),
        sd((bsz, seq, GLA_HEADS * GLA_DK), F32), sd((bsz, seq, GLA_HEADS * GLA_DK), F32),
        sd((bsz, seq, GLA_HEADS * GLA_DV), F32), sd((bsz, seq, GLA_HEADS * GLA_DV), F32),
        sd((bsz, seq, LANE), F32),
    ]
    return pl.pallas_call(
        _pre_mix_kernel, out_shape=out_shape, grid=(nt, bsz), in_specs=in_specs, out_specs=out_specs,
        compiler_params=_cparams(("arbitrary", "arbitrary")), name="pre_mix",
    )(x, mods, norm_g, wts["w_in"], wts["q_norm"], wts["kv_norm"], wts["w_uq"], wts["w_kT"], wts["w_v"],
      wts["w_krT"], tabs["cq"], tabs["sq"], tabs["ckT"], tabs["skT"])


def _flash_kernel(q_ref, kT_ref, v_ref, o_ref, *, n_groups, kv_group):
    ctx = kT_ref.shape[2] - 1
    qs = [q_ref[0, hh] for hh in range(2)]

    def attend(carry, chunks):
        kts = [jnp.concatenate([kT_ref[0, hh, c] for c in chunks], axis=1) for hh in range(2)]
        vs = [jnp.concatenate([v_ref[0, hh, c * TM:(c + 1) * TM, :] for c in chunks], axis=0) for hh in range(2)]
        ss = [_dot(qs[hh], kts[hh]) for hh in range(2)]
        out = []
        for hh in range(2):
            m_new = jnp.max(ss[hh], axis=-1, keepdims=True)
            if carry is not None:
                m, acc = carry[2 * hh], carry[2 * hh + 1]
                m_new = jnp.maximum(m, m_new)
            p = jnp.exp2(ss[hh] - m_new).astype(BF16)
            pv = _dot(p, vs[hh])
            out += [m_new, pv if carry is None else jnp.exp2(m - m_new) * acc + pv]
        return tuple(out)

    carry = attend(None, [ctx] + list(range(kv_group)))
    for j in range(1, n_groups):
        carry = attend(carry, list(range(j * kv_group, (j + 1) * kv_group)))
    o_ref[0] = jnp.concatenate([carry[2 * hh + 1][:, :MLA_V] / carry[2 * hh + 1][:, MLA_V:MLA_V + 1]
                                for hh in range(2)], axis=-1)


def mla_attention(q, kT, v):
    bsz, nh, seq, _ = q.shape
    nt = seq // TM
    t_latent = seq - CTX_LEN
    kv_group = KV_GROUP if (nt - 1) % KV_GROUP == 0 else 2
    assert t_latent % TQ == 0 and (nt - 1) % kv_group == 0
    kv_specs = [pl.BlockSpec((1, 2, nt, HEAD_PAD, TM), lambda b, h, i: (b, h, 0, 0, 0)),
                pl.BlockSpec((1, 2, seq, HEAD_PAD), lambda b, h, i: (b, h, 0, 0))]
    sem = ("arbitrary", "arbitrary", "arbitrary")
    o_latent = pl.pallas_call(
        functools.partial(_flash_kernel, n_groups=(nt - 1) // kv_group, kv_group=kv_group),
        out_shape=jax.ShapeDtypeStruct((bsz, t_latent, nh * MLA_V), F32),
        grid=(bsz, nh // 2, t_latent // TQ),
        in_specs=[pl.BlockSpec((1, 2, TQ, HEAD_PAD), lambda b, h, i: (b, h, i, 0))] + kv_specs,
        out_specs=pl.BlockSpec((1, TQ, 2 * MLA_V), lambda b, h, i: (b, i, h)),
        compiler_params=_cparams(sem), name="mla_attention_latent",
    )(q, kT, v)
    o_context = pl.pallas_call(
        functools.partial(_flash_kernel, n_groups=0, kv_group=0),
        out_shape=jax.ShapeDtypeStruct((bsz, CTX_LEN, nh * MLA_V), F32),
        grid=(bsz, nh // 2, 1),
        in_specs=[pl.BlockSpec((1, 2, TM, HEAD_PAD), lambda b, h, i: (b, h, nt - 1, 0)),
                  pl.BlockSpec((1, 2, 1, HEAD_PAD, TM), lambda b, h, i: (b, h, nt - 1, 0, 0)),
                  pl.BlockSpec((1, 2, TM, HEAD_PAD), lambda b, h, i: (b, h, nt - 1, 0))],
        out_specs=pl.BlockSpec((1, TM, 2 * MLA_V), lambda b, h, i: (b, 0, h)),
        compiler_params=_cparams(sem), name="mla_attention_context",
    )(q, kT, v)
    return o_latent, o_context


def _split3(x):
    h1 = x.astype(BF16)
    r1 = x - h1.astype(F32)
    h2 = r1.astype(BF16)
    h3 = (r1 - h2.astype(F32)).astype(BF16)
    return h1, h2, h3


def _dot_ta(a, b):
    return lax.dot_general(a, b, (((0,), (0,)), ((), ())), preferred_element_type=F32)


def _tri(n, d):
    r = lax.broadcasted_iota(jnp.int32, (n, n), 0)
    c = lax.broadcasted_iota(jnp.int32, (n, n), 1)
    return jnp.where(c <= r if d == 0 else c >= r, 1.0, 0.0).astype(BF16)


def _lane_head(width, per_head):
    return lax.broadcasted_iota(jnp.int32, (1, width), 1) // per_head


def _stack_heads(x, n_heads, per_head):
    lh = _lane_head(x.shape[1], per_head) % n_heads
    return jnp.concatenate([jnp.where(lh == h, x, 0.0) for h in range(n_heads)], axis=0)


def _chunk_maps(n_chunks, n_latent):
    fwd = lambda n: (n + n_latent) % n_chunks
    bwd = lambda n: n_chunks - 1 - n
    return fwd, bwd


def _job_specs(widths, n_chunks, n_latent, seq, nb):
    per = RC // HALO
    last = seq // HALO - 1
    specs = []
    for cmap in _chunk_maps(n_chunks, n_latent):
        chunk = lambda w, cmap=cmap: pl.BlockSpec((nb, RC, w), lambda b, n: (b, cmap(n), 0))
        if widths[0] is not None:
            w0 = widths[0]
            specs += [chunk(w0),
                      pl.BlockSpec((nb, HALO, w0), lambda b, n, cmap=cmap: (b, jnp.maximum(cmap(n) * per - 1, 0), 0)),
                      pl.BlockSpec((nb, HALO, w0), lambda b, n, cmap=cmap: (b, jnp.minimum((cmap(n) + 1) * per, last), 0))]
        specs += [chunk(w) for w in widths[1:]]
    return specs


def _lockstep(jobs):
    live = [(job, next(job)) for job in jobs]
    while live:
        groups = {}
        for ji, (_, reqs) in enumerate(live):
            for ri, (fn, a, b) in enumerate(reqs):
                groups.setdefault((fn, id(b)), []).append((ji, ri, a, b))
        results = [[None] * len(reqs) for _, reqs in live]
        for (fn, _), members in groups.items():
            if fn in (_dot, _dot_t) and len(members) > 1:
                out = fn(jnp.concatenate([a for _, _, a, _ in members], axis=0), members[0][3])
                row = 0
                for ji, ri, a, _ in members:
                    results[ji][ri] = out[row:row + a.shape[0]]
                    row += a.shape[0]
            else:
                for ji, ri, a, b in members:
                    results[ji][ri] = fn(a, b)
        nxt = []
        for (job, _), res in zip(live, results):
            try:
                nxt.append((job, job.send(res)))
            except StopIteration:
                pass
        live = nxt


def _gdn_job(d, c, n_chunks, n_latent, qkv, prev, nxt, misc, conv_w_ref, esel,
             alog_ref, dtb_ref, gsum, mask_ref, s_ref, s_idx, store_o):
    hw = GDN_HEADS * GDN_DK
    prev_ok = (c != 0) & (c != n_latent)
    next_ok = (c != n_latent - 1) & (c != n_chunks - 1)
    x = _silu(_conv3(jnp.concatenate([prev, qkv, nxt], axis=0), conv_w_ref, slice(None), prev_ok, next_ok))
    q, k, v = x[:, :hw], x[:, hw:2 * hw], x[:, 2 * hw:]
    sq = [q * q, k * k]
    hi = [a.astype(BF16) for a in sq]
    lo = [(a - h.astype(F32)).astype(BF16) for a, h in zip(sq, hi)]
    r = yield [(_dot, a, gsum) for a in hi + lo]
    q = q * lax.rsqrt(r[0] + r[2] + NORM_EPS) * GDN_DK ** -0.5
    k = k * lax.rsqrt(r[1] + r[3] + NORM_EPS)

    lane = lax.broadcasted_iota(jnp.int32, (1, LANE), 1)
    gates = jnp.where((lane >= M_A) & (lane < M_B),
                      -jnp.exp(alog_ref[...]) * _softplus(misc + dtb_ref[...]), jax.nn.sigmoid(misc))
    r = yield [(_dot, p, esel[d]) for p in _split3(gates)]
    gb = r[0] + r[1] + r[2]
    g, beta = gb[:, :hw], gb[:, hw:]
    tri = _tri(RC, d)
    r = yield [(_dot, tri, p) for p in _split3(g)]
    gc = r[0] + r[1] + r[2]
    total = gc[RC - 1:RC] if d == 0 else gc[0:1]
    eg = jnp.exp(gc)
    kb = k * beta
    rhs = jnp.concatenate([v * beta, kb * eg], axis=1)
    qd = q * eg
    kend = k * jnp.exp(total - gc)

    bd = lambda m: _stack_heads(m, GDN_HEADS, RC)
    ks = bd(k).astype(BF16)
    kk, qk = yield [(_dot_t, kb.astype(BF16), ks), (_dot_t, q.astype(BF16), ks)]
    eye, strict = mask_ref[0], mask_ref[1 + d]
    gc_j = jnp.sum(eye * gc, axis=0, keepdims=True)
    decay = jnp.exp(jnp.minimum(gc - gc_j, 0.0))
    lmat = strict * (kk * decay)
    amat = (strict + eye) * (qk * decay)

    n_sq = int(math.log2(RC)) - 1
    p = lmat
    t = eye - lmat
    (p,) = yield [(_dot, p, bd(p))]
    for _ in range(n_sq - 1):
        (r,) = yield [(_dot, jnp.concatenate([p, t], axis=0), bd(p))]
        p, t = r[:RC], t + r[RC:]
    (r,) = yield [(_dot, t, bd(p))]
    t = t + r
    (uw,) = yield [(_dot, t.astype(BF16), bd(rhs).astype(BF16))]
    u, w = uw[:, :hw], uw[:, hw:]

    s = s_ref[s_idx]
    (ws,) = yield [(_dot, jnp.concatenate([w, qd], axis=0).astype(BF16), s.astype(BF16))]
    v_new = u - ws[:RC]
    o_intra, kv = yield [(_dot, amat.astype(BF16), bd(v_new).astype(BF16)), (_dot_ta, kend, v_new)]
    store_o(ws[RC:] + o_intra)
    rh = lax.broadcasted_iota(jnp.int32, (hw, hw), 0) // GDN_DK
    ch = lax.broadcasted_iota(jnp.int32, (hw, hw), 1) // GDN_DV
    s_ref[s_idx] = s * jnp.exp(total) + jnp.where(rh == ch, kv, 0.0)


def _store_at(ref, bi):
    def store(val):
        ref[bi] = val
    return store


_GLA_LEVELS = int(math.log2(RC))


def _gdn_masks():
    i = np.arange(RC)[:, None]
    j = (np.arange(GDN_HEADS * RC) % RC)[None, :]
    return np.stack([i == j, j < i, j > i]).astype(np.float32)


def _gla_pair_masks():
    i = np.arange(RC)[:, None]
    j = (np.arange(GLA_HEADS * RC) % RC)[None, :]
    masks = np.zeros((2, _GLA_LEVELS + 1, RC, GLA_HEADS * RC), np.float32)
    for d in range(2):
        masks[d, 0] = i == j
        for ls in range(_GLA_LEVELS):
            s = 1 << ls
            same = (i >> (ls + 1)) == (j >> (ls + 1))
            i_late, j_late = (i & (2 * s - 1)) >= s, (j & (2 * s - 1)) >= s
            masks[d, 1 + ls] = same & ((i_late & ~j_late) if d == 0 else (~i_late & j_late))
    return masks


def _block_mid_rows(b, ls, d):
    n, width = b.shape
    s = 1 << ls
    off = s - 1 if d == 0 else s
    if 2 * s >= SUBLANE:
        return jnp.concatenate([jnp.broadcast_to(b[m + off:m + off + 1], (2 * s, width))
                                for m in range(0, n, 2 * s)], axis=0)
    pos = lax.broadcasted_iota(jnp.int32, (n, 1), 0) & (2 * s - 1)
    out = None
    for r in range(2 * s):
        shift = (r - off) % n
        rolled = b if shift == 0 else pltpu.roll(b, shift, axis=0)
        out = rolled if out is None else jnp.where(pos == r, rolled, out)
    return out


def _gla_job(d, q, k, v, misc, wgk, bgk_ref, pair_ref, st_ref, s_idx, store_o):
    q = q * GLA_DK ** -0.5
    (gk,) = yield [(_dot, misc.astype(BF16), wgk[d])]
    la = -_softplus(-(gk + bgk_ref[d])) * (1.0 / GLA_NORMALIZER)
    tri = _tri(RC, d)
    r = yield [(_dot, tri, p) for p in _split3(la)]
    b = r[0] + r[1] + r[2]
    blast = b[RC - 1:RC] if d == 0 else b[0:1]

    lhs, rhs = [q.astype(BF16)], [_stack_heads(k, GLA_HEADS, GLA_DK).astype(BF16)]
    for ls in range(_GLA_LEVELS):
        ref = _block_mid_rows(b, ls, d)
        lhs.append((q * jnp.exp(jnp.minimum(b - ref, 0.0))).astype(BF16))
        rhs.append(_stack_heads(k * jnp.exp(jnp.minimum(ref - b, 0.0)), GLA_HEADS, GLA_DK).astype(BF16))
    r = yield [(_dot_t, a, bb) for a, bb in zip(lhs, rhs)]
    amat = pair_ref[d, 0] * r[0]
    for lvl in range(1, _GLA_LEVELS + 1):
        amat = amat + pair_ref[d, lvl] * r[lvl]

    st = st_ref[s_idx]
    kend = k * jnp.exp(blast - b)
    o_intra, o_inter, vk = yield [
        (_dot, amat.astype(BF16), _stack_heads(v, GLA_HEADS, GLA_DV).astype(BF16)),
        (_dot_t, (q * jnp.exp(b)).astype(BF16), st.astype(BF16)),
        (_dot_ta, v, kend)]
    store_o(o_intra + o_inter)
    rh = lax.broadcasted_iota(jnp.int32, st.shape, 0) // GLA_DV
    ch = lax.broadcasted_iota(jnp.int32, st.shape, 1) // GLA_DK
    st_ref[s_idx] = st * jnp.exp(blast) + jnp.where(rh == ch, vk, 0.0)


def _mixers_kernel(*refs, n_chunks, n_latent):
    (qkv_f, prev_f, next_f, misc_f, qkv_r, prev_r, next_r, misc_r,
     lq_f, lk_f, lv_f, lq_r, lk_r, lv_r,
     conv_w, esel, alog, dtb, gsum, gdn_mask, wgk, bgk, gla_pairs,
     gdn_f, gdn_r, gla_f, gla_r, s_ref, st_ref) = refs
    n = pl.program_id(1)

    @pl.when(n == 0)
    def _():
        s_ref[...] = jnp.zeros_like(s_ref)
        st_ref[...] = jnp.zeros_like(st_ref)

    fwd, bwd = _chunk_maps(n_chunks, n_latent)
    esel, wgk = [esel[0], esel[1]], [wgk[0], wgk[1]]
    gdn_consts = (conv_w, esel, alog, dtb, gsum[...], gdn_mask, s_ref)
    jobs = []
    for bi in range(qkv_f.shape[0]):
        jobs.append(_gdn_job(0, fwd(n), n_chunks, n_latent, qkv_f[bi], prev_f[bi], next_f[bi], misc_f[bi],
                             *gdn_consts, 2 * bi, _store_at(gdn_f, bi)))
        jobs.append(_gdn_job(1, bwd(n), n_chunks, n_latent, qkv_r[bi], prev_r[bi], next_r[bi], misc_r[bi],
                             *gdn_consts, 2 * bi + 1, _store_at(gdn_r, bi)))
    for bi in range(lq_f.shape[0]):
        jobs.append(_gla_job(0, lq_f[bi], lk_f[bi], lv_f[bi], misc_f[bi], wgk, bgk, gla_pairs, st_ref, 2 * bi,
                             _store_at(gla_f, bi)))
        jobs.append(_gla_job(1, lq_r[bi], lk_r[bi], lv_r[bi], misc_r[bi], wgk, bgk, gla_pairs, st_ref, 2 * bi + 1,
                             _store_at(gla_r, bi)))
    _lockstep(jobs)


def recurrent_mixers(gqkv, misc, lq, lk, lv, wts):
    bsz, seq, hk = lq.shape
    hv = lv.shape[2]
    hw = GDN_HEADS * GDN_DV
    n_chunks = seq // RC
    n_latent = (seq - CTX_LEN) // RC
    nb = RNB if bsz % RNB == 0 else 1
    const = lambda *shape: pl.BlockSpec(shape, lambda b, n: (0,) * len(shape))
    fwd, bwd = _chunk_maps(n_chunks, n_latent)
    out = lambda w: jax.ShapeDtypeStruct((bsz, seq, w), F32)
    out_spec = lambda w, cmap: pl.BlockSpec((nb, RC, w), lambda b, n: (b, cmap(n), 0))
    return pl.pallas_call(
        functools.partial(_mixers_kernel, n_chunks=n_chunks, n_latent=n_latent),
        out_shape=[out(hw), out(hw), out(hv), out(hv)],
        grid=(bsz // nb, n_chunks),
        in_specs=_job_specs((GDN_QKV, LANE), n_chunks, n_latent, seq, nb)
        + _job_specs((None, hk, hk, hv), n_chunks, n_latent, seq, nb)
        + [const(3, GDN_QKV), const(2, LANE, 2 * hw), const(1, LANE), const(1, LANE), const(hw, hw),
           const(3, RC, hw), const(2, LANE, hk), const(2, 1, hk), const(2, _GLA_LEVELS + 1, RC, GLA_HEADS * RC)],
        out_specs=[out_spec(hw, fwd), out_spec(hw, bwd), out_spec(hv, fwd), out_spec(hv, bwd)],
        scratch_shapes=[pltpu.VMEM((2 * nb, hw, hw), F32), pltpu.VMEM((2 * nb, hv, hk), F32)],
        compiler_params=_cparams(("arbitrary", "arbitrary")),
        name="recurrent_mixers",
    )(gqkv, gqkv, gqkv, misc, gqkv, gqkv, gqkv, misc, lq, lk, lv, lq, lk, lv,
      wts["gdn_conv_w"], wts["gdn_esel"], wts["gdn_alog"], wts["gdn_dtb"], wts["gsum"], jnp.asarray(_gdn_masks()),
      wts["gla_wgk"], wts["gla_bgk"], jnp.asarray(_gla_pair_masks()))


def _group_mean_sq(x, gmat):
    xsq = x * x
    hi = xsq.astype(BF16)
    lo = (xsq - hi.astype(F32)).astype(BF16)
    return _dot(hi, gmat) + _dot(lo, gmat)


def _post_mix_kernel(x_ref, mod_ref, mla_l_ref, mla_c_ref, gdn_f_ref, gdn_r_ref, gz_ref, gla_f_ref, gla_r_ref,
                     lg_ref, gmat_ref, gdn_g_ref, gla_g_ref, w_out_ref, post_g_ref, o_ref):
    is_ctx = pl.program_id(1) == pl.num_programs(1) - 1
    mla = jnp.where(is_ctx, mla_c_ref[0], mla_l_ref[0])
    gmat = gmat_ref[...]
    gdn = gdn_f_ref[0] + gdn_r_ref[0]
    gdn = gdn * lax.rsqrt(_group_mean_sq(gdn, gmat) + NORM_EPS) * gdn_g_ref[...] * _silu(gz_ref[0])
    gla = gla_f_ref[0] + gla_r_ref[0]
    gla = gla * lax.rsqrt(_group_mean_sq(gla, gmat) + NORM_EPS) * gla_g_ref[...] * _silu(lg_ref[0])
    merged = jnp.concatenate([mla, gdn, gla], axis=-1).astype(BF16)
    y = _dot(merged, w_out_ref[...])
    gate = mod_ref[0, 2:3, :]
    o_ref[0] = x_ref[0] + gate * _rms(y, post_g_ref[...])


def post_mix(x, mods, mla_l, mla_c, gdn_f, gdn_r, gz, gla_f, gla_r, lg, wts, post_g):
    bsz, seq, d = x.shape
    nt = seq // TM
    const = lambda *shape: pl.BlockSpec(shape, lambda b, i: (0,) * len(shape))
    tile = lambda w: pl.BlockSpec((1, TM, w), lambda b, i: (b, i, 0))
    hw = GDN_HEADS * GDN_DV
    hm = MLA_HEADS * MLA_V
    return pl.pallas_call(
        _post_mix_kernel,
        out_shape=jax.ShapeDtypeStruct(x.shape, F32),
        grid=(bsz, nt),
        in_specs=[tile(d),
                  pl.BlockSpec((1, N_MOD, d), lambda b, i: (jnp.where(i == nt - 1, bsz, b), 0, 0)),
                  pl.BlockSpec((1, TM, hm), lambda b, i: (b, jnp.minimum(i, nt - 2), 0)),
                  pl.BlockSpec((1, TM, hm), lambda b, i: (b, 0, 0)),
                  tile(hw), tile(hw), tile(hw), tile(hw), tile(hw), tile(hw),
                  const(hw, hw), const(1, hw), const(1, hw), const(D_MIX, d), const(1, d)],
        out_specs=tile(d),
        compiler_params=_cparams(("arbitrary", "arbitrary")),
        name="post_mix",
    )(x, mods, mla_l, mla_c, gdn_f, gdn_r, gz, gla_f, gla_r, lg, wts["gmat"], wts["gdn_norm"], wts["gla_norm"],
      wts["w_out"], post_g)


def _conv3(z, w_ref, cols, prev_ok, next_ok, pad=0.0):
    rows = z.shape[0]
    n = rows - 2 * HALO
    z = jnp.concatenate([jnp.where(prev_ok, z[:HALO], pad), z[HALO:HALO + n],
                         jnp.where(next_ok, z[HALO + n:], pad)], axis=0)
    zm = pltpu.roll(z, 1, axis=0)[HALO:HALO + n]
    zp = pltpu.roll(z, rows - 1, axis=0)[HALO:HALO + n]
    zc = z[HALO:HALO + n]
    return zm * w_ref[0:1, cols] + zc * w_ref[1:2, cols] + zp * w_ref[2:3, cols]


def _ffn_kernel(x_ref, xp_ref, xn_ref, mod_ref, pre_g_ref, w_in_ref, b_in_ref, cw_ref, cb_ref,
                w_out_ref, post_g_ref, o_ref, *, nt):
    i = pl.program_id(1)
    prev_ok = (i > 0) & (i < nt - 1)
    next_ok = i < nt - 2
    x = x_ref[0]
    xx = jnp.concatenate([xp_ref[0], x, xn_ref[0]], axis=0)
    shift = mod_ref[0, 3:4, :]
    scale = mod_ref[0, 4:5, :]
    hb = (_rms(xx, pre_g_ref[...]) * (1.0 + scale) + shift).astype(BF16)
    n_chunks = FFN_HIDDEN // FFN_CHUNK
    cols = [(slice(j * FFN_CHUNK, (j + 1) * FFN_CHUNK),
             slice(FFN_HIDDEN + j * FFN_CHUNK, FFN_HIDDEN + (j + 1) * FFN_CHUNK)) for j in range(n_chunks)]
    acts = []
    for ca, cg in cols:
        za, zg = _dot(hb, w_in_ref[:, ca]), _dot(hb, w_in_ref[:, cg])
        bias = lambda c: cb_ref[:, c] + b_in_ref[:, c] * (cw_ref[0:1, c] + cw_ref[1:2, c] + cw_ref[2:3, c])
        a = _conv3(za, cw_ref, ca, prev_ok, next_ok, -b_in_ref[:, ca]) + bias(ca)
        g = _conv3(zg, cw_ref, cg, prev_ok, next_ok, -b_in_ref[:, cg]) + bias(cg)
        acts.append((a * _silu(g)).astype(BF16))
    acc = _dot(jnp.concatenate(acts, axis=1), w_out_ref[...])
    gate = mod_ref[0, 5:6, :]
    o_ref[0] = x + gate * _rms(acc, post_g_ref[...])


def conv_ffn(x, mods, pre_g, wts, post_g, keep_context=True):
    bsz, seq, d = x.shape
    nt = seq // TM
    n_out = nt if keep_context else nt - 1
    nb = seq // HALO
    per = TM // HALO
    const = lambda *shape: pl.BlockSpec(shape, lambda b, i: (0,) * len(shape))
    tile = pl.BlockSpec((1, TM, d), lambda b, i: (b, i, 0))
    return pl.pallas_call(
        functools.partial(_ffn_kernel, nt=nt),
        out_shape=jax.ShapeDtypeStruct((bsz, n_out * TM, d), F32),
        grid=(bsz, n_out),
        in_specs=[tile,
                  pl.BlockSpec((1, HALO, d), lambda b, i: (b, jnp.maximum(i * per - 1, 0), 0)),
                  pl.BlockSpec((1, HALO, d), lambda b, i: (b, jnp.minimum((i + 1) * per, nb - 1), 0)),
                  pl.BlockSpec((1, N_MOD, d), lambda b, i: (jnp.where(i == nt - 1, bsz, b), 0, 0)),
                  const(1, d), const(d, 2 * FFN_HIDDEN), const(1, 2 * FFN_HIDDEN),
                  const(3, 2 * FFN_HIDDEN), const(1, 2 * FFN_HIDDEN), const(FFN_HIDDEN, d), const(1, d)],
        out_specs=tile,
        compiler_params=_cparams(("arbitrary", "arbitrary")),
        name="conv_ffn",
    )(x, x, x, mods, pre_g, wts["ffn_w_in"], wts["ffn_b_in"], wts["ffn_conv_w"], wts["ffn_conv_b"],
      wts["ffn_w_out"], post_g)


def _rope_tables(t_latent):
    n = MLA_ROPE // 4
    t = jnp.arange(t_latent)
    row = (t // GRID_W).astype(F32)
    col = (t % GRID_W).astype(F32)
    inv = ROPE_THETA ** (-jnp.arange(n, dtype=F32) / n)
    ang = jnp.stack([row[:, None] * inv, col[:, None] * inv], axis=1)
    cos, sin = jnp.cos(ang), jnp.sin(ang)
    c32 = jnp.stack([cos, cos], axis=2).reshape(t_latent, MLA_ROPE)
    s32 = jnp.stack([-sin, sin], axis=2).reshape(t_latent, MLA_ROPE)
    c32 = jnp.concatenate([c32, jnp.ones((CTX_LEN, MLA_ROPE), F32)], axis=0)
    s32 = jnp.concatenate([s32, jnp.zeros((CTX_LEN, MLA_ROPE), F32)], axis=0)
    seq = CTX_LEN + t_latent
    qscale = (MLA_NOPE + MLA_ROPE) ** -0.5 * math.log2(math.e)
    pad = HEAD_PAD - MLA_NOPE - MLA_ROPE
    cq = jnp.concatenate([jnp.ones((seq, MLA_NOPE), F32), c32, jnp.zeros((seq, pad), F32)], axis=1) * qscale
    sq = jnp.concatenate([jnp.zeros((seq, MLA_NOPE), F32), s32, jnp.zeros((seq, pad), F32)], axis=1) * qscale
    return {"cq": jnp.tile(cq, (1, MLA_HEADS)), "sq": jnp.tile(sq, (1, MLA_HEADS)),
            "ckT": c32.T, "skT": s32.T}


_ROPE_PARTNER = np.arange(MLA_ROPE) ^ (MLA_ROPE // 4)


def _prep_layer(i, w_in, mla_q_norm, mla_w_uq, mla_kv_norm, mla_w_ukv, gdn_conv_w, gdn_a_log, gdn_dt_bias,
                gdn_norm, gla_w_gk, gla_b_gk, gla_norm, w_out, ffn_w_in, ffn_b_in, ffn_conv_w, ffn_conv_b,
                ffn_w_out):
    o = IN_OFFS
    w = w_in[i]
    d = w.shape[0]
    piece = lambda k: w[:, o[k]:o[k + 1]]
    misc_pad = LANE - (M_LR + 2 * GLA_GATE_RANK)
    w_in_p = jnp.concatenate(
        [piece(0), piece(1), piece(3), piece(4), piece(7), piece(8), piece(9), piece(10),
         piece(2), piece(5), piece(6), piece(11), jnp.zeros((d, misc_pad), F32)], axis=1).astype(BF16)
    kr = piece(2)
    w_krT = jnp.concatenate([kr, kr[:, _ROPE_PARTNER]], axis=1).T.astype(BF16)

    hq = MLA_NOPE + MLA_ROPE
    uq = mla_w_uq[i].reshape(MLA_Q_RANK, MLA_HEADS, hq)
    plain = jnp.concatenate([uq, jnp.zeros((MLA_Q_RANK, MLA_HEADS, HEAD_PAD - hq), F32)], axis=2)
    w_uq = plain.reshape(MLA_Q_RANK, -1).astype(BF16)

    ukv = mla_w_ukv[i].reshape(MLA_KV_RANK, MLA_HEADS, MLA_NOPE + MLA_V)
    w_kT = ukv[:, :, :MLA_NOPE].reshape(MLA_KV_RANK, -1).T.astype(BF16)
    w_v = jnp.concatenate([ukv[:, :, MLA_NOPE:], jnp.zeros((MLA_KV_RANK, MLA_HEADS, HEAD_PAD - MLA_V), F32)],
                          axis=2).reshape(MLA_KV_RANK, -1).astype(BF16)

    hw = GDN_HEADS * GDN_DV
    lane_head = np.arange(hw) // GDN_DV
    same_head = (lane_head[:, None] == lane_head[None, :]).astype(np.float32)
    esel = np.zeros((2, LANE, 2 * hw), np.float32)
    for dd in range(2):
        for h in range(GDN_HEADS):
            esel[dd, M_A + dd * GDN_HEADS + h, h * GDN_DV:(h + 1) * GDN_DV] = 1.0
            esel[dd, M_B + dd * GDN_HEADS + h, hw + h * GDN_DV:hw + (h + 1) * GDN_DV] = 1.0
    wgk = jnp.zeros((2, LANE, GLA_HEADS * GLA_DK), F32)
    for dd in range(2):
        r0 = M_LR + dd * GLA_GATE_RANK
        wgk = wgk.at[dd, r0:r0 + GLA_GATE_RANK].set(gla_w_gk[i, dd])
    return {
        "w_in": w_in_p, "w_krT": w_krT, "w_uq": w_uq, "w_kT": w_kT, "w_v": w_v,
        "q_norm": mla_q_norm[i][None], "kv_norm": mla_kv_norm[i][None],
        "gdn_conv_w": gdn_conv_w[i], "gdn_esel": jnp.asarray(esel).astype(BF16),
        "gdn_alog": jnp.zeros((1, LANE), F32).at[0, M_A:M_B].set(gdn_a_log[i].reshape(-1)),
        "gdn_dtb": jnp.zeros((1, LANE), F32).at[0, M_A:M_B].set(gdn_dt_bias[i].reshape(-1)),
        "gsum": jnp.asarray(same_head).astype(BF16),
        "gla_wgk": wgk.astype(BF16), "gla_bgk": gla_b_gk[i][:, None, :],
        "gmat": jnp.asarray(same_head / GDN_DV).astype(BF16), "gdn_norm": jnp.tile(gdn_norm[i], GDN_HEADS)[None],
        "gla_norm": jnp.tile(gla_norm[i], GLA_HEADS)[None], "w_out": w_out[i].astype(BF16),
        "ffn_w_in": ffn_w_in[i].astype(BF16), "ffn_b_in": ffn_b_in[i][None], "ffn_conv_w": ffn_conv_w[i],
        "ffn_conv_b": ffn_conv_b[i][None], "ffn_w_out": ffn_w_out[i].astype(BF16),
    }


def kernel(x, c, ctx, c_ctx, w_ada, b_ada, norm_mix_pre, norm_mix_post, norm_ffn_pre, norm_ffn_post,
           w_in, mla_q_norm, mla_w_uq, mla_kv_norm, mla_w_ukv, gdn_conv_w, gdn_a_log, gdn_dt_bias,
           gdn_norm, gla_w_gk, gla_b_gk, gla_norm, w_out, ffn_w_in, ffn_b_in, ffn_conv_w, ffn_conv_b,
           ffn_w_out):
    bsz, t_latent, d = x.shape
    assert ctx.shape[1] == CTX_LEN and t_latent % TM == 0 and t_latent % GRID_W == 0
    depth = w_ada.shape[0]
    xs = jnp.concatenate([x, ctx], axis=1)
    mod_rows = -(-(bsz + 1) // SUBLANE) * SUBLANE
    cvec = jnp.concatenate([c, c_ctx[None], jnp.zeros((mod_rows - bsz - 1, d), F32)], axis=0)
    mods_all = ada_modulation(cvec, w_ada, b_ada).reshape(depth, mod_rows, N_MOD, d)
    tabs = _rope_tables(t_latent)
    for i in range(depth):
        mods = mods_all[i]
        wts = _prep_layer(i, w_in, mla_q_norm, mla_w_uq, mla_kv_norm, mla_w_ukv, gdn_conv_w, gdn_a_log,
                          gdn_dt_bias, gdn_norm, gla_w_gk, gla_b_gk, gla_norm, w_out, ffn_w_in, ffn_b_in,
                          ffn_conv_w, ffn_conv_b, ffn_w_out)
        q, kT, v, gqkv, gz, lq, lk, lv, lg, misc = pre_mix(xs, mods, norm_mix_pre[i][None], wts, tabs)
        mla_l, mla_c = mla_attention(q, kT, v)
        gdn_f, gdn_r, gla_f, gla_r = recurrent_mixers(gqkv, misc, lq, lk, lv, wts)
        xs = post_mix(xs, mods, mla_l, mla_c, gdn_f, gdn_r, gz, gla_f, gla_r, lg, wts, norm_mix_post[i][None])
        xs = conv_ffn(xs, mods, norm_ffn_pre[i][None], wts, norm_ffn_post[i][None], keep_context=i < depth - 1)
    return xs
```

```python
import functools
import math

import numpy as np
import jax
import jax.numpy as jnp
from jax import lax
from jax.experimental import pallas as pl
from jax.experimental.pallas import tpu as pltpu

F32 = jnp.float32
BF16 = jnp.bfloat16

GRID_W = 64
CTX_LEN = 256
N_MOD = 6
NORM_EPS = 1e-6

MLA_HEADS = 8
MLA_Q_RANK = 384
MLA_KV_RANK = 256
MLA_NOPE = 64
MLA_ROPE = 32
MLA_V = 64
ROPE_THETA = 10000.0

GDN_HEADS = 4
GDN_DK = 64
GDN_DV = 64

GLA_HEADS = 4
GLA_DK = 32
GLA_DV = 64
GLA_GATE_RANK = 16
GLA_NORMALIZER = 16.0

FFN_HIDDEN = 2560
D_MIX = MLA_HEADS * MLA_V + GDN_HEADS * GDN_DV + GLA_HEADS * GLA_DV
GDN_QKV = GDN_HEADS * (2 * GDN_DK + GDN_DV)
IN_SIZES = (MLA_Q_RANK, MLA_KV_RANK, MLA_ROPE,
            GDN_QKV, GDN_HEADS * GDN_DV, 2 * GDN_HEADS, 2 * GDN_HEADS,
            GLA_HEADS * GLA_DK, GLA_HEADS * GLA_DK, GLA_HEADS * GLA_DV, GLA_HEADS * GLA_DV,
            2 * GLA_GATE_RANK)
IN_OFFS = tuple(int(s) for s in np.cumsum((0,) + IN_SIZES))

LANE = 128
SUBLANE = 8
TM = CTX_LEN
TQ = 2 * TM
KV_GROUP = 8
HALO = SUBLANE
HEAD_PAD = LANE
FFN_CHUNK = 512
RC = 64
RNB = 8
assert RC == GDN_DK
VMEM_LIMIT = 56 * 1024 * 1024

P_CQ = 0
P_CKV = P_CQ + MLA_Q_RANK
P_GQKV = P_CKV + MLA_KV_RANK
P_GZ = P_GQKV + GDN_QKV
P_LQ = P_GZ + GDN_HEADS * GDN_DV
P_LK = P_LQ + GLA_HEADS * GLA_DK
P_LV = P_LK + GLA_HEADS * GLA_DK
P_LG = P_LV + GLA_HEADS * GLA_DV
P_MISC = P_LG + GLA_HEADS * GLA_DV
P_TOTAL = P_MISC + LANE
M_KR = 0
M_A = MLA_ROPE
M_B = M_A + 2 * GDN_HEADS
M_LR = M_B + 2 * GDN_HEADS


def _cparams(sem, vmem=VMEM_LIMIT):
    return pltpu.CompilerParams(dimension_semantics=sem, vmem_limit_bytes=vmem)


def _rms(x, g):
    return x * lax.rsqrt(jnp.mean(x * x, axis=-1, keepdims=True) + NORM_EPS) * g


def _silu(x):
    return x * jax.nn.sigmoid(x)


def _softplus(x):
    return jnp.maximum(x, 0.0) + jnp.log1p(jnp.exp(-jnp.abs(x)))


def _dot(a, b):
    return jnp.dot(a, b, preferred_element_type=F32)


def _dot_t(a, b):
    return lax.dot_general(a, b, (((1,), (1,)), ((), ())), preferred_element_type=F32)


def _ada_kernel(c_ref, w_ref, b_ref, o_ref):
    o_ref[0] = _dot(_silu(c_ref[...]), w_ref[0]) + b_ref[0]


def ada_modulation(cvec, w_ada, b_ada):
    depth, d, n = w_ada.shape
    rows = cvec.shape[0]
    tn = n // 4
    return pl.pallas_call(
        _ada_kernel,
        out_shape=jax.ShapeDtypeStruct((depth, rows, n), F32),
        grid=(depth, n // tn),
        in_specs=[pl.BlockSpec((rows, d), lambda l, j: (0, 0)),
                  pl.BlockSpec((1, d, tn), lambda l, j: (l, 0, j)),
                  pl.BlockSpec((1, 1, tn), lambda l, j: (l, 0, j))],
        out_specs=pl.BlockSpec((1, rows, tn), lambda l, j: (l, 0, j)),
        compiler_params=_cparams(("arbitrary", "arbitrary")),
        name="ada_modulation",
    )(cvec, w_ada, b_ada.reshape(depth, 1, n))


def _tile_rows(refs, n_x, tile_axis):
    if n_x == 1:
        return refs[0][0], refs[1:]
    is_ctx = pl.program_id(tile_axis) == pl.num_programs(tile_axis) - 1
    return jnp.where(is_ctx, refs[1][0], refs[0][0]), refs[2:]


def _x_specs(d, split, nt, batch_tile):
    if not split:
        return [pl.BlockSpec((1, TM, d), lambda *g: (*batch_tile(*g), 0))]
    return [pl.BlockSpec((1, TM, d), lambda *g: (batch_tile(*g)[0], jnp.minimum(batch_tile(*g)[1], nt - 2), 0)),
            pl.BlockSpec((1, TM, d), lambda *g: (batch_tile(*g)[0], 0, 0))]


def _pre_mix_kernel(*refs, n_x):
    x, refs = _tile_rows(refs, n_x, 0)
    (mod_ref, g_ref, w_in_ref, qn_ref, kvn_ref, w_uq_ref, w_kT_ref, w_v_ref, w_krT_ref, cq_ref, sq_ref,
     ckT_ref, skT_ref, q_ref, kT_ref, v_ref, gqkv_ref, gz_ref, lq_ref, lk_ref, lv_ref, lg_ref, misc_ref) = refs
    shift = mod_ref[0, 0:1, :]
    scale = mod_ref[0, 1:2, :]
    h = _rms(x, g_ref[...]) * (1.0 + scale) + shift
    hb = h.astype(BF16)
    p = _dot(hb, w_in_ref[...])

    gqkv_ref[0] = p[:, P_GQKV:P_GZ]
    gz_ref[0] = p[:, P_GZ:P_LQ]
    lq_ref[0] = p[:, P_LQ:P_LK]
    lk_ref[0] = p[:, P_LK:P_LV]
    lv_ref[0] = p[:, P_LV:P_LG]
    lg_ref[0] = p[:, P_LG:P_MISC]
    misc_ref[0] = p[:, P_MISC:P_TOTAL]

    cqn = _rms(p[:, P_CQ:P_CKV], qn_ref[...]).astype(BF16)
    ckvn = _rms(p[:, P_CKV:P_GQKV], kvn_ref[...]).astype(BF16)
    nq = MLA_HEADS * HEAD_PAD
    qp = _dot(cqn, w_uq_ref[...])
    n = MLA_ROPE // 4
    lane = lax.broadcasted_iota(jnp.int32, (1, nq), 1)
    first_half = (((lane % HEAD_PAD) - MLA_NOPE) & n) == 0
    partner = jnp.where(first_half, pltpu.roll(qp, nq - n, axis=1), pltpu.roll(qp, n, axis=1))
    q = qp * cq_ref[...] + partner * sq_ref[...]
    vf = _dot(ckvn, w_v_ref[...])
    one_lane = lax.broadcasted_iota(jnp.int32, (1, HEAD_PAD), 1) == MLA_V
    kTn = _dot_t(w_kT_ref[...], ckvn)
    krT = _dot_t(w_krT_ref[...], hb)
    kr = (krT[:MLA_ROPE] * ckT_ref[...] + krT[MLA_ROPE:] * skT_ref[...]).astype(BF16)
    zpad = jnp.zeros((HEAD_PAD - MLA_NOPE - MLA_ROPE, kr.shape[1]), BF16)
    for hd in range(MLA_HEADS):
        q_ref[0, hd] = q[:, hd * HEAD_PAD:(hd + 1) * HEAD_PAD].astype(BF16)
        vh = vf[:, hd * HEAD_PAD:(hd + 1) * HEAD_PAD]
        v_ref[0, hd] = jnp.where(one_lane, 1.0, vh).astype(BF16)
        kT_ref[0, hd, 0, 0:MLA_NOPE, :] = kTn[hd * MLA_NOPE:(hd + 1) * MLA_NOPE].astype(BF16)
        kT_ref[0, hd, 0, MLA_NOPE:MLA_NOPE + MLA_ROPE, :] = kr
        kT_ref[0, hd, 0, MLA_NOPE + MLA_ROPE:, :] = zpad


def pre_mix(xs, mods, norm_g, wts, tabs):
    bsz, _, d = xs[0].shape
    seq = sum(a.shape[1] for a in xs)
    nt = seq // TM
    const = lambda *shape: pl.BlockSpec(shape, lambda i, b: (0,) * len(shape))
    tile = lambda w: pl.BlockSpec((1, TM, w), lambda i, b: (b, i, 0))
    nq = MLA_HEADS * HEAD_PAD
    in_specs = _x_specs(d, len(xs) == 2, nt, lambda i, b: (b, i)) + [
        pl.BlockSpec((1, N_MOD, d), lambda i, b: (jnp.where(i == nt - 1, bsz, b), 0, 0)),
        const(1, d),
        const(d, P_TOTAL),
        const(1, MLA_Q_RANK),
        const(1, MLA_KV_RANK),
        const(MLA_Q_RANK, nq),
        const(MLA_HEADS * MLA_NOPE, MLA_KV_RANK),
        const(MLA_KV_RANK, nq),
        const(2 * MLA_ROPE, d),
        pl.BlockSpec((TM, nq), lambda i, b: (i, 0)),
        pl.BlockSpec((TM, nq), lambda i, b: (i, 0)),
        pl.BlockSpec((MLA_ROPE, TM), lambda i, b: (0, i)),
        pl.BlockSpec((MLA_ROPE, TM), lambda i, b: (0, i)),
    ]
    head_rows = pl.BlockSpec((1, MLA_HEADS, TM, HEAD_PAD), lambda i, b: (b, 0, i, 0))
    out_specs = [
        head_rows,
        pl.BlockSpec((1, MLA_HEADS, 1, HEAD_PAD, TM), lambda i, b: (b, 0, i, 0, 0)),
        head_rows,
        tile(GDN_QKV), tile(GDN_HEADS * GDN_DV),
        tile(GLA_HEADS * GLA_DK), tile(GLA_HEADS * GLA_DK),
        tile(GLA_HEADS * GLA_DV), tile(GLA_HEADS * GLA_DV),
        tile(LANE),
    ]
    sd = jax.ShapeDtypeStruct
    out_shape = [
        sd((bsz, MLA_HEADS, seq, HEAD_PAD), BF16),
        sd((bsz, MLA_HEADS, nt, HEAD_PAD, TM), BF16),
        sd((bsz, MLA_HEADS, seq, HEAD_PAD), BF16),
        sd((bsz, seq, GDN_QKV), F32), sd((bsz, seq, GDN_HEADS * GDN_DV), F32),
        sd((bsz, seq, GLA_HEADS * GLA_DK), F32), sd((bsz, seq, GLA_HEADS * GLA_DK), F32),
        sd((bsz, seq, GLA_HEADS * GLA_DV), F32), sd((bsz, seq, GLA_HEADS * GLA_DV), F32),
        sd((bsz, seq, LANE), F32),
    ]
    return pl.pallas_call(
        functools.partial(_pre_mix_kernel, n_x=len(xs)), out_shape=out_shape, grid=(nt, bsz),
        in_specs=in_specs, out_specs=out_specs,
        compiler_params=_cparams(("arbitrary", "arbitrary")), name="pre_mix",
    )(*xs, mods, norm_g, wts["w_in"], wts["q_norm"], wts["kv_norm"], wts["w_uq"], wts["w_kT"], wts["w_v"],
      wts["w_krT"], tabs["cq"], tabs["sq"], tabs["ckT"], tabs["skT"])


def _flash_kernel(q_ref, kT_ref, v_ref, o_ref, *, n_groups, kv_group):
    ctx = kT_ref.shape[2] - 1
    qs = [q_ref[0, hh] for hh in range(2)]

    def attend(carry, chunks):
        kts = [jnp.concatenate([kT_ref[0, hh, c] for c in chunks], axis=1) for hh in range(2)]
        vs = [jnp.concatenate([v_ref[0, hh, c * TM:(c + 1) * TM, :] for c in chunks], axis=0) for hh in range(2)]
        ss = [_dot(qs[hh], kts[hh]) for hh in range(2)]
        out = []
        for hh in range(2):
            m_new = jnp.max(ss[hh], axis=-1, keepdims=True)
            if carry is not None:
                m, acc = carry[2 * hh], carry[2 * hh + 1]
                m_new = jnp.maximum(m, m_new)
            p = jnp.exp2(ss[hh] - m_new).astype(BF16)
            pv = _dot(p, vs[hh])
            out += [m_new, pv if carry is None else jnp.exp2(m - m_new) * acc + pv]
        return tuple(out)

    carry = attend(None, [ctx] + list(range(kv_group)))
    for j in range(1, n_groups):
        carry = attend(carry, list(range(j * kv_group, (j + 1) * kv_group)))
    o_ref[0] = jnp.concatenate([carry[2 * hh + 1][:, :MLA_V] / carry[2 * hh + 1][:, MLA_V:MLA_V + 1]
                                for hh in range(2)], axis=-1)


def mla_attention(q, kT, v):
    bsz, nh, seq, _ = q.shape
    nt = seq // TM
    t_latent = seq - CTX_LEN
    kv_group = KV_GROUP if (nt - 1) % KV_GROUP == 0 else 2
    assert t_latent % TQ == 0 and (nt - 1) % kv_group == 0
    kv_specs = [pl.BlockSpec((1, 2, nt, HEAD_PAD, TM), lambda b, h, i: (b, h, 0, 0, 0)),
                pl.BlockSpec((1, 2, seq, HEAD_PAD), lambda b, h, i: (b, h, 0, 0))]
    sem = ("arbitrary", "arbitrary", "arbitrary")
    o_latent = pl.pallas_call(
        functools.partial(_flash_kernel, n_groups=(nt - 1) // kv_group, kv_group=kv_group),
        out_shape=jax.ShapeDtypeStruct((bsz, t_latent, nh * MLA_V), F32),
        grid=(bsz, nh // 2, t_latent // TQ),
        in_specs=[pl.BlockSpec((1, 2, TQ, HEAD_PAD), lambda b, h, i: (b, h, i, 0))] + kv_specs,
        out_specs=pl.BlockSpec((1, TQ, 2 * MLA_V), lambda b, h, i: (b, i, h)),
        compiler_params=_cparams(sem), name="mla_attention_latent",
    )(q, kT, v)
    o_context = pl.pallas_call(
        functools.partial(_flash_kernel, n_groups=0, kv_group=0),
        out_shape=jax.ShapeDtypeStruct((bsz, CTX_LEN, nh * MLA_V), F32),
        grid=(bsz, nh // 2, 1),
        in_specs=[pl.BlockSpec((1, 2, TM, HEAD_PAD), lambda b, h, i: (b, h, nt - 1, 0)),
                  pl.BlockSpec((1, 2, 1, HEAD_PAD, TM), lambda b, h, i: (b, h, nt - 1, 0, 0)),
                  pl.BlockSpec((1, 2, TM, HEAD_PAD), lambda b, h, i: (b, h, nt - 1, 0))],
        out_specs=pl.BlockSpec((1, TM, 2 * MLA_V), lambda b, h, i: (b, 0, h)),
        compiler_params=_cparams(sem), name="mla_attention_context",
    )(q, kT, v)
    return o_latent, o_context


def _split3(x):
    h1 = x.astype(BF16)
    r1 = x - h1.astype(F32)
    h2 = r1.astype(BF16)
    h3 = (r1 - h2.astype(F32)).astype(BF16)
    return h1, h2, h3


def _dot_ta(a, b):
    return lax.dot_general(a, b, (((0,), (0,)), ((), ())), preferred_element_type=F32)


def _tri(n, d):
    r = lax.broadcasted_iota(jnp.int32, (n, n), 0)
    c = lax.broadcasted_iota(jnp.int32, (n, n), 1)
    return jnp.where(c <= r if d == 0 else c >= r, 1.0, 0.0).astype(BF16)


def _lane_head(width, per_head):
    return lax.broadcasted_iota(jnp.int32, (1, width), 1) // per_head


def _stack_heads(x, n_heads, per_head):
    lh = _lane_head(x.shape[1], per_head) % n_heads
    return jnp.concatenate([jnp.where(lh == h, x, 0.0) for h in range(n_heads)], axis=0)


def _chunk_maps(n_chunks, n_latent):
    fwd = lambda n: (n + n_latent) % n_chunks
    bwd = lambda n: n_chunks - 1 - n
    return fwd, bwd


def _job_specs(widths, n_chunks, n_latent, seq, nb):
    per = RC // HALO
    last = seq // HALO - 1
    specs = []
    for cmap in _chunk_maps(n_chunks, n_latent):
        chunk = lambda w, cmap=cmap: pl.BlockSpec((nb, RC, w), lambda b, n: (b, cmap(n), 0))
        if widths[0] is not None:
            w0 = widths[0]
            specs += [chunk(w0),
                      pl.BlockSpec((nb, HALO, w0), lambda b, n, cmap=cmap: (b, jnp.maximum(cmap(n) * per - 1, 0), 0)),
                      pl.BlockSpec((nb, HALO, w0), lambda b, n, cmap=cmap: (b, jnp.minimum((cmap(n) + 1) * per, last), 0))]
        specs += [chunk(w) for w in widths[1:]]
    return specs


def _lockstep(jobs):
    live = [(job, next(job)) for job in jobs]
    while live:
        groups = {}
        for ji, (_, reqs) in enumerate(live):
            for ri, (fn, a, b) in enumerate(reqs):
                groups.setdefault((fn, id(b)), []).append((ji, ri, a, b))
        results = [[None] * len(reqs) for _, reqs in live]
        for (fn, _), members in groups.items():
            if fn in (_dot, _dot_t) and len(members) > 1:
                out = fn(jnp.concatenate([a for _, _, a, _ in members], axis=0), members[0][3])
                row = 0
                for ji, ri, a, _ in members:
                    results[ji][ri] = out[row:row + a.shape[0]]
                    row += a.shape[0]
            else:
                for ji, ri, a, b in members:
                    results[ji][ri] = fn(a, b)
        nxt = []
        for (job, _), res in zip(live, results):
            try:
                nxt.append((job, job.send(res)))
            except StopIteration:
                pass
        live = nxt


def _gdn_job(d, c, n_chunks, n_latent, qkv, prev, nxt, misc, conv_w_ref, esel,
             alog_ref, dtb_ref, gsum, mask_ref, s_ref, s_idx, store_o):
    hw = GDN_HEADS * GDN_DK
    prev_ok = (c != 0) & (c != n_latent)
    next_ok = (c != n_latent - 1) & (c != n_chunks - 1)
    x = _silu(_conv3(jnp.concatenate([prev, qkv, nxt], axis=0), conv_w_ref, slice(None), prev_ok, next_ok))
    q, k, v = x[:, :hw], x[:, hw:2 * hw], x[:, 2 * hw:]
    sq = [q * q, k * k]
    hi = [a.astype(BF16) for a in sq]
    lo = [(a - h.astype(F32)).astype(BF16) for a, h in zip(sq, hi)]
    r = yield [(_dot, a, gsum) for a in hi + lo]
    q = q * lax.rsqrt(r[0] + r[2] + NORM_EPS) * GDN_DK ** -0.5
    k = k * lax.rsqrt(r[1] + r[3] + NORM_EPS)

    lane = lax.broadcasted_iota(jnp.int32, (1, LANE), 1)
    gates = jnp.where((lane >= M_A) & (lane < M_B),
                      -jnp.exp(alog_ref[...]) * _softplus(misc + dtb_ref[...]), jax.nn.sigmoid(misc))
    r = yield [(_dot, p, esel[d]) for p in _split3(gates)]
    gb = r[0] + r[1] + r[2]
    g, beta = gb[:, :hw], gb[:, hw:]
    tri = _tri(RC, d)
    r = yield [(_dot, tri, p) for p in _split3(g)]
    gc = r[0] + r[1] + r[2]
    total = gc[RC - 1:RC] if d == 0 else gc[0:1]
    eg = jnp.exp(gc)
    kb = k * beta
    rhs = jnp.concatenate([v * beta, kb * eg], axis=1)
    qd = q * eg
    kend = k * jnp.exp(total - gc)

    bd = lambda m: _stack_heads(m, GDN_HEADS, RC)
    ks = bd(k).astype(BF16)
    kk, qk = yield [(_dot_t, kb.astype(BF16), ks), (_dot_t, q.astype(BF16), ks)]
    eye, strict = mask_ref[0], mask_ref[1 + d]
    gc_j = jnp.sum(eye * gc, axis=0, keepdims=True)
    decay = jnp.exp(jnp.minimum(gc - gc_j, 0.0))
    lmat = strict * (kk * decay)
    amat = (strict + eye) * (qk * decay)

    n_sq = int(math.log2(RC)) - 1
    p = lmat
    t = eye - lmat
    (p,) = yield [(_dot, p, bd(p))]
    for _ in range(n_sq - 1):
        (r,) = yield [(_dot, jnp.concatenate([p, t], axis=0), bd(p))]
        p, t = r[:RC], t + r[RC:]
    (r,) = yield [(_dot, t, bd(p))]
    t = t + r
    (uw,) = yield [(_dot, t.astype(BF16), bd(rhs).astype(BF16))]
    u, w = uw[:, :hw], uw[:, hw:]

    s = s_ref[s_idx]
    (ws,) = yield [(_dot, jnp.concatenate([w, qd], axis=0).astype(BF16), s.astype(BF16))]
    v_new = u - ws[:RC]
    o_intra, kv = yield [(_dot, amat.astype(BF16), bd(v_new).astype(BF16)), (_dot_ta, kend, v_new)]
    store_o(ws[RC:] + o_intra)
    rh = lax.broadcasted_iota(jnp.int32, (hw, hw), 0) // GDN_DK
    ch = lax.broadcasted_iota(jnp.int32, (hw, hw), 1) // GDN_DV
    s_ref[s_idx] = s * jnp.exp(total) + jnp.where(rh == ch, kv, 0.0)


def _store_at(ref, bi):
    def store(val):
        ref[bi] = val
    return store


_GLA_LEVELS = int(math.log2(RC))


def _gdn_masks():
    i = np.arange(RC)[:, None]
    j = (np.arange(GDN_HEADS * RC) % RC)[None, :]
    return np.stack([i == j, j < i, j > i]).astype(np.float32)


def _gla_pair_masks():
    i = np.arange(RC)[:, None]
    j = (np.arange(GLA_HEADS * RC) % RC)[None, :]
    masks = np.zeros((2, _GLA_LEVELS + 1, RC, GLA_HEADS * RC), np.float32)
    for d in range(2):
        masks[d, 0] = i == j
        for ls in range(_GLA_LEVELS):
            s = 1 << ls
            same = (i >> (ls + 1)) == (j >> (ls + 1))
            i_late, j_late = (i & (2 * s - 1)) >= s, (j & (2 * s - 1)) >= s
            masks[d, 1 + ls] = same & ((i_late & ~j_late) if d == 0 else (~i_late & j_late))
    return masks


def _block_mid_rows(b, ls, d):
    n, width = b.shape
    s = 1 << ls
    off = s - 1 if d == 0 else s
    if 2 * s >= SUBLANE:
        return jnp.concatenate([jnp.broadcast_to(b[m + off:m + off + 1], (2 * s, width))
                                for m in range(0, n, 2 * s)], axis=0)
    pos = lax.broadcasted_iota(jnp.int32, (n, 1), 0) & (2 * s - 1)
    out = None
    for r in range(2 * s):
        shift = (r - off) % n
        rolled = b if shift == 0 else pltpu.roll(b, shift, axis=0)
        out = rolled if out is None else jnp.where(pos == r, rolled, out)
    return out


def _gla_job(d, q, k, v, misc, wgk, bgk_ref, pair_ref, st_ref, s_idx, store_o):
    q = q * GLA_DK ** -0.5
    (gk,) = yield [(_dot, misc.astype(BF16), wgk[d])]
    la = -_softplus(-(gk + bgk_ref[d])) * (1.0 / GLA_NORMALIZER)
    tri = _tri(RC, d)
    r = yield [(_dot, tri, p) for p in _split3(la)]
    b = r[0] + r[1] + r[2]
    blast = b[RC - 1:RC] if d == 0 else b[0:1]

    lhs, rhs = [q.astype(BF16)], [_stack_heads(k, GLA_HEADS, GLA_DK).astype(BF16)]
    for ls in range(_GLA_LEVELS):
        ref = _block_mid_rows(b, ls, d)
        lhs.append((q * jnp.exp(jnp.minimum(b - ref, 0.0))).astype(BF16))
        rhs.append(_stack_heads(k * jnp.exp(jnp.minimum(ref - b, 0.0)), GLA_HEADS, GLA_DK).astype(BF16))
    r = yield [(_dot_t, a, bb) for a, bb in zip(lhs, rhs)]
    amat = pair_ref[d, 0] * r[0]
    for lvl in range(1, _GLA_LEVELS + 1):
        amat = amat + pair_ref[d, lvl] * r[lvl]

    st = st_ref[s_idx]
    kend = k * jnp.exp(blast - b)
    o_intra, o_inter, vk = yield [
        (_dot, amat.astype(BF16), _stack_heads(v, GLA_HEADS, GLA_DV).astype(BF16)),
        (_dot_t, (q * jnp.exp(b)).astype(BF16), st.astype(BF16)),
        (_dot_ta, v, kend)]
    store_o(o_intra + o_inter)
    rh = lax.broadcasted_iota(jnp.int32, st.shape, 0) // GLA_DV
    ch = lax.broadcasted_iota(jnp.int32, st.shape, 1) // GLA_DK
    st_ref[s_idx] = st * jnp.exp(blast) + jnp.where(rh == ch, vk, 0.0)


def _mixers_kernel(*refs, n_chunks, n_latent):
    (qkv_f, prev_f, next_f, misc_f, qkv_r, prev_r, next_r, misc_r,
     lq_f, lk_f, lv_f, lq_r, lk_r, lv_r,
     conv_w, esel, alog, dtb, gsum, gdn_mask, wgk, bgk, gla_pairs,
     gdn_f, gdn_r, gla_f, gla_r, s_ref, st_ref) = refs
    n = pl.program_id(1)

    @pl.when(n == 0)
    def _():
        s_ref[...] = jnp.zeros_like(s_ref)
        st_ref[...] = jnp.zeros_like(st_ref)

    fwd, bwd = _chunk_maps(n_chunks, n_latent)
    esel, wgk = [esel[0], esel[1]], [wgk[0], wgk[1]]
    gdn_consts = (conv_w, esel, alog, dtb, gsum[...], gdn_mask, s_ref)
    jobs = []
    for bi in range(qkv_f.shape[0]):
        jobs.append(_gdn_job(0, fwd(n), n_chunks, n_latent, qkv_f[bi], prev_f[bi], next_f[bi], misc_f[bi],
                             *gdn_consts, 2 * bi, _store_at(gdn_f, bi)))
        jobs.append(_gdn_job(1, bwd(n), n_chunks, n_latent, qkv_r[bi], prev_r[bi], next_r[bi], misc_r[bi],
                             *gdn_consts, 2 * bi + 1, _store_at(gdn_r, bi)))
    for bi in range(lq_f.shape[0]):
        jobs.append(_gla_job(0, lq_f[bi], lk_f[bi], lv_f[bi], misc_f[bi], wgk, bgk, gla_pairs, st_ref, 2 * bi,
                             _store_at(gla_f, bi)))
        jobs.append(_gla_job(1, lq_r[bi], lk_r[bi], lv_r[bi], misc_r[bi], wgk, bgk, gla_pairs, st_ref, 2 * bi + 1,
                             _store_at(gla_r, bi)))
    _lockstep(jobs)


def recurrent_mixers(gqkv, misc, lq, lk, lv, wts):
    bsz, seq, hk = lq.shape
    hv = lv.shape[2]
    hw = GDN_HEADS * GDN_DV
    n_chunks = seq // RC
    n_latent = (seq - CTX_LEN) // RC
    nb = RNB if bsz % RNB == 0 else 1
    const = lambda *shape: pl.BlockSpec(shape, lambda b, n: (0,) * len(shape))
    fwd, bwd = _chunk_maps(n_chunks, n_latent)
    out = lambda w: jax.ShapeDtypeStruct((bsz, seq, w), F32)
    out_spec = lambda w, cmap: pl.BlockSpec((nb, RC, w), lambda b, n: (b, cmap(n), 0))
    return pl.pallas_call(
        functools.partial(_mixers_kernel, n_chunks=n_chunks, n_latent=n_latent),
        out_shape=[out(hw), out(hw), out(hv), out(hv)],
        grid=(bsz // nb, n_chunks),
        in_specs=_job_specs((GDN_QKV, LANE), n_chunks, n_latent, seq, nb)
        + _job_specs((None, hk, hk, hv), n_chunks, n_latent, seq, nb)
        + [const(3, GDN_QKV), const(2, LANE, 2 * hw), const(1, LANE), const(1, LANE), const(hw, hw),
           const(3, RC, hw), const(2, LANE, hk), const(2, 1, hk), const(2, _GLA_LEVELS + 1, RC, GLA_HEADS * RC)],
        out_specs=[out_spec(hw, fwd), out_spec(hw, bwd), out_spec(hv, fwd), out_spec(hv, bwd)],
        scratch_shapes=[pltpu.VMEM((2 * nb, hw, hw), F32), pltpu.VMEM((2 * nb, hv, hk), F32)],
        compiler_params=_cparams(("arbitrary", "arbitrary")),
        name="recurrent_mixers",
    )(gqkv, gqkv, gqkv, misc, gqkv, gqkv, gqkv, misc, lq, lk, lv, lq, lk, lv,
      wts["gdn_conv_w"], wts["gdn_esel"], wts["gdn_alog"], wts["gdn_dtb"], wts["gsum"], jnp.asarray(_gdn_masks()),
      wts["gla_wgk"], wts["gla_bgk"], jnp.asarray(_gla_pair_masks()))


def _group_mean_sq(x, gmat):
    xsq = x * x
    hi = xsq.astype(BF16)
    lo = (xsq - hi.astype(F32)).astype(BF16)
    return _dot(hi, gmat) + _dot(lo, gmat)


def _post_mix_kernel(*refs, n_x):
    x, refs = _tile_rows(refs, n_x, 1)
    (mod_ref, mla_l_ref, mla_c_ref, gdn_f_ref, gdn_r_ref, gz_ref, gla_f_ref, gla_r_ref,
     lg_ref, gmat_ref, gdn_g_ref, gla_g_ref, w_out_ref, post_g_ref, o_ref) = refs
    is_ctx = pl.program_id(1) == pl.num_programs(1) - 1
    mla = jnp.where(is_ctx, mla_c_ref[0], mla_l_ref[0])
    gmat = gmat_ref[...]
    gdn = gdn_f_ref[0] + gdn_r_ref[0]
    gdn = gdn * lax.rsqrt(_group_mean_sq(gdn, gmat) + NORM_EPS) * gdn_g_ref[...] * _silu(gz_ref[0])
    gla = gla_f_ref[0] + gla_r_ref[0]
    gla = gla * lax.rsqrt(_group_mean_sq(gla, gmat) + NORM_EPS) * gla_g_ref[...] * _silu(lg_ref[0])
    merged = jnp.concatenate([mla, gdn, gla], axis=-1).astype(BF16)
    y = _dot(merged, w_out_ref[...])
    gate = mod_ref[0, 2:3, :]
    o_ref[0] = x + gate * _rms(y, post_g_ref[...])


def post_mix(xs, mods, mla_l, mla_c, gdn_f, gdn_r, gz, gla_f, gla_r, lg, wts, post_g):
    bsz, _, d = xs[0].shape
    seq = sum(a.shape[1] for a in xs)
    nt = seq // TM
    const = lambda *shape: pl.BlockSpec(shape, lambda b, i: (0,) * len(shape))
    tile = lambda w: pl.BlockSpec((1, TM, w), lambda b, i: (b, i, 0))
    hw = GDN_HEADS * GDN_DV
    hm = MLA_HEADS * MLA_V
    return pl.pallas_call(
        functools.partial(_post_mix_kernel, n_x=len(xs)),
        out_shape=jax.ShapeDtypeStruct((bsz, seq, d), F32),
        grid=(bsz, nt),
        in_specs=_x_specs(d, len(xs) == 2, nt, lambda b, i: (b, i))
        + [pl.BlockSpec((1, N_MOD, d), lambda b, i: (jnp.where(i == nt - 1, bsz, b), 0, 0)),
                  pl.BlockSpec((1, TM, hm), lambda b, i: (b, jnp.minimum(i, nt - 2), 0)),
                  pl.BlockSpec((1, TM, hm), lambda b, i: (b, 0, 0)),
                  tile(hw), tile(hw), tile(hw), tile(hw), tile(hw), tile(hw),
                  const(hw, hw), const(1, hw), const(1, hw), const(D_MIX, d), const(1, d)],
        out_specs=tile(d),
        compiler_params=_cparams(("arbitrary", "arbitrary")),
        name="post_mix",
    )(*xs, mods, mla_l, mla_c, gdn_f, gdn_r, gz, gla_f, gla_r, lg, wts["gmat"], wts["gdn_norm"], wts["gla_norm"],
      wts["w_out"], post_g)


def _conv3(z, w_ref, cols, prev_ok, next_ok, pad=0.0):
    rows = z.shape[0]
    n = rows - 2 * HALO
    z = jnp.concatenate([jnp.where(prev_ok, z[:HALO], pad), z[HALO:HALO + n],
                         jnp.where(next_ok, z[HALO + n:], pad)], axis=0)
    zm = pltpu.roll(z, 1, axis=0)[HALO:HALO + n]
    zp = pltpu.roll(z, rows - 1, axis=0)[HALO:HALO + n]
    zc = z[HALO:HALO + n]
    return zm * w_ref[0:1, cols] + zc * w_ref[1:2, cols] + zp * w_ref[2:3, cols]


def _ffn_kernel(x_ref, xp_ref, xn_ref, mod_ref, pre_g_ref, w_in_ref, b_in_ref, cw_ref, cb_ref,
                w_out_ref, post_g_ref, o_ref, *, nt):
    i = pl.program_id(1)
    prev_ok = (i > 0) & (i < nt - 1)
    next_ok = i < nt - 2
    x = x_ref[0]
    xx = jnp.concatenate([xp_ref[0], x, xn_ref[0]], axis=0)
    shift = mod_ref[0, 3:4, :]
    scale = mod_ref[0, 4:5, :]
    hb = (_rms(xx, pre_g_ref[...]) * (1.0 + scale) + shift).astype(BF16)
    n_chunks = FFN_HIDDEN // FFN_CHUNK
    cols = [(slice(j * FFN_CHUNK, (j + 1) * FFN_CHUNK),
             slice(FFN_HIDDEN + j * FFN_CHUNK, FFN_HIDDEN + (j + 1) * FFN_CHUNK)) for j in range(n_chunks)]
    acts = []
    for ca, cg in cols:
        za, zg = _dot(hb, w_in_ref[:, ca]), _dot(hb, w_in_ref[:, cg])
        bias = lambda c: cb_ref[:, c] + b_in_ref[:, c] * (cw_ref[0:1, c] + cw_ref[1:2, c] + cw_ref[2:3, c])
        a = _conv3(za, cw_ref, ca, prev_ok, next_ok, -b_in_ref[:, ca]) + bias(ca)
        g = _conv3(zg, cw_ref, cg, prev_ok, next_ok, -b_in_ref[:, cg]) + bias(cg)
        acts.append((a * _silu(g)).astype(BF16))
    acc = _dot(jnp.concatenate(acts, axis=1), w_out_ref[...])
    gate = mod_ref[0, 5:6, :]
    o_ref[0] = x + gate * _rms(acc, post_g_ref[...])


def conv_ffn(x, mods, pre_g, wts, post_g, keep_context=True):
    bsz, seq, d = x.shape
    nt = seq // TM
    n_out = nt if keep_context else nt - 1
    nb = seq // HALO
    per = TM // HALO
    const = lambda *shape: pl.BlockSpec(shape, lambda b, i: (0,) * len(shape))
    tile = pl.BlockSpec((1, TM, d), lambda b, i: (b, i, 0))
    return pl.pallas_call(
        functools.partial(_ffn_kernel, nt=nt),
        out_shape=jax.ShapeDtypeStruct((bsz, n_out * TM, d), F32),
        grid=(bsz, n_out),
        in_specs=[tile,
                  pl.BlockSpec((1, HALO, d), lambda b, i: (b, jnp.maximum(i * per - 1, 0), 0)),
                  pl.BlockSpec((1, HALO, d), lambda b, i: (b, jnp.minimum((i + 1) * per, nb - 1), 0)),
                  pl.BlockSpec((1, N_MOD, d), lambda b, i: (jnp.where(i == nt - 1, bsz, b), 0, 0)),
                  const(1, d), const(d, 2 * FFN_HIDDEN), const(1, 2 * FFN_HIDDEN),
                  const(3, 2 * FFN_HIDDEN), const(1, 2 * FFN_HIDDEN), const(FFN_HIDDEN, d), const(1, d)],
        out_specs=tile,
        compiler_params=_cparams(("arbitrary", "arbitrary")),
        name="conv_ffn",
    )(x, x, x, mods, pre_g, wts["ffn_w_in"], wts["ffn_b_in"], wts["ffn_conv_w"], wts["ffn_conv_b"],
      wts["ffn_w_out"], post_g)


def _rope_tables(t_latent):
    n = MLA_ROPE // 4
    t = jnp.arange(t_latent)
    row = (t // GRID_W).astype(F32)
    col = (t % GRID_W).astype(F32)
    inv = ROPE_THETA ** (-jnp.arange(n, dtype=F32) / n)
    ang = jnp.stack([row[:, None] * inv, col[:, None] * inv], axis=1)
    cos, sin = jnp.cos(ang), jnp.sin(ang)
    c32 = jnp.stack([cos, cos], axis=2).reshape(t_latent, MLA_ROPE)
    s32 = jnp.stack([-sin, sin], axis=2).reshape(t_latent, MLA_ROPE)
    c32 = jnp.concatenate([c32, jnp.ones((CTX_LEN, MLA_ROPE), F32)], axis=0)
    s32 = jnp.concatenate([s32, jnp.zeros((CTX_LEN, MLA_ROPE), F32)], axis=0)
    seq = CTX_LEN + t_latent
    qscale = (MLA_NOPE + MLA_ROPE) ** -0.5 * math.log2(math.e)
    pad = HEAD_PAD - MLA_NOPE - MLA_ROPE
    cq = jnp.concatenate([jnp.ones((seq, MLA_NOPE), F32), c32, jnp.zeros((seq, pad), F32)], axis=1) * qscale
    sq = jnp.concatenate([jnp.zeros((seq, MLA_NOPE), F32), s32, jnp.zeros((seq, pad), F32)], axis=1) * qscale
    return {"cq": jnp.tile(cq, (1, MLA_HEADS)), "sq": jnp.tile(sq, (1, MLA_HEADS)),
            "ckT": c32.T, "skT": s32.T}


_ROPE_PARTNER = np.arange(MLA_ROPE) ^ (MLA_ROPE // 4)


def _prep_layer(i, w_in, mla_q_norm, mla_w_uq, mla_kv_norm, mla_w_ukv, gdn_conv_w, gdn_a_log, gdn_dt_bias,
                gdn_norm, gla_w_gk, gla_b_gk, gla_norm, w_out, ffn_w_in, ffn_b_in, ffn_conv_w, ffn_conv_b,
                ffn_w_out):
    o = IN_OFFS
    w = w_in[i]
    d = w.shape[0]
    piece = lambda k: w[:, o[k]:o[k + 1]]
    misc_pad = LANE - (M_LR + 2 * GLA_GATE_RANK)
    w_in_p = jnp.concatenate(
        [piece(0), piece(1), piece(3), piece(4), piece(7), piece(8), piece(9), piece(10),
         piece(2), piece(5), piece(6), piece(11), jnp.zeros((d, misc_pad), F32)], axis=1).astype(BF16)
    kr = piece(2)
    w_krT = jnp.concatenate([kr, kr[:, _ROPE_PARTNER]], axis=1).T.astype(BF16)

    hq = MLA_NOPE + MLA_ROPE
    uq = mla_w_uq[i].reshape(MLA_Q_RANK, MLA_HEADS, hq)
    plain = jnp.concatenate([uq, jnp.zeros((MLA_Q_RANK, MLA_HEADS, HEAD_PAD - hq), F32)], axis=2)
    w_uq = plain.reshape(MLA_Q_RANK, -1).astype(BF16)

    ukv = mla_w_ukv[i].reshape(MLA_KV_RANK, MLA_HEADS, MLA_NOPE + MLA_V)
    w_kT = ukv[:, :, :MLA_NOPE].reshape(MLA_KV_RANK, -1).T.astype(BF16)
    w_v = jnp.concatenate([ukv[:, :, MLA_NOPE:], jnp.zeros((MLA_KV_RANK, MLA_HEADS, HEAD_PAD - MLA_V), F32)],
                          axis=2).reshape(MLA_KV_RANK, -1).astype(BF16)

    hw = GDN_HEADS * GDN_DV
    lane_head = np.arange(hw) // GDN_DV
    same_head = (lane_head[:, None] == lane_head[None, :]).astype(np.float32)
    esel = np.zeros((2, LANE, 2 * hw), np.float32)
    for dd in range(2):
        for h in range(GDN_HEADS):
            esel[dd, M_A + dd * GDN_HEADS + h, h * GDN_DV:(h + 1) * GDN_DV] = 1.0
            esel[dd, M_B + dd * GDN_HEADS + h, hw + h * GDN_DV:hw + (h + 1) * GDN_DV] = 1.0
    wgk = jnp.zeros((2, LANE, GLA_HEADS * GLA_DK), F32)
    for dd in range(2):
        r0 = M_LR + dd * GLA_GATE_RANK
        wgk = wgk.at[dd, r0:r0 + GLA_GATE_RANK].set(gla_w_gk[i, dd])
    return {
        "w_in": w_in_p, "w_krT": w_krT, "w_uq": w_uq, "w_kT": w_kT, "w_v": w_v,
        "q_norm": mla_q_norm[i][None], "kv_norm": mla_kv_norm[i][None],
        "gdn_conv_w": gdn_conv_w[i], "gdn_esel": jnp.asarray(esel).astype(BF16),
        "gdn_alog": jnp.zeros((1, LANE), F32).at[0, M_A:M_B].set(gdn_a_log[i].reshape(-1)),
        "gdn_dtb": jnp.zeros((1, LANE), F32).at[0, M_A:M_B].set(gdn_dt_bias[i].reshape(-1)),
        "gsum": jnp.asarray(same_head).astype(BF16),
        "gla_wgk": wgk.astype(BF16), "gla_bgk": gla_b_gk[i][:, None, :],
        "gmat": jnp.asarray(same_head / GDN_DV).astype(BF16), "gdn_norm": jnp.tile(gdn_norm[i], GDN_HEADS)[None],
        "gla_norm": jnp.tile(gla_norm[i], GLA_HEADS)[None], "w_out": w_out[i].astype(BF16),
        "ffn_w_in": ffn_w_in[i].astype(BF16), "ffn_b_in": ffn_b_in[i][None], "ffn_conv_w": ffn_conv_w[i],
        "ffn_conv_b": ffn_conv_b[i][None], "ffn_w_out": ffn_w_out[i].astype(BF16),
    }


def kernel(x, c, ctx, c_ctx, w_ada, b_ada, norm_mix_pre, norm_mix_post, norm_ffn_pre, norm_ffn_post,
           w_in, mla_q_norm, mla_w_uq, mla_kv_norm, mla_w_ukv, gdn_conv_w, gdn_a_log, gdn_dt_bias,
           gdn_norm, gla_w_gk, gla_b_gk, gla_norm, w_out, ffn_w_in, ffn_b_in, ffn_conv_w, ffn_conv_b,
           ffn_w_out):
    bsz, t_latent, d = x.shape
    assert ctx.shape[1] == CTX_LEN and t_latent % TM == 0 and t_latent % GRID_W == 0
    depth = w_ada.shape[0]
    xs = (x, ctx)
    mod_rows = -(-(bsz + 1) // SUBLANE) * SUBLANE
    cvec = jnp.concatenate([c, c_ctx[None], jnp.zeros((mod_rows - bsz - 1, d), F32)], axis=0)
    mods_all = ada_modulation(cvec, w_ada, b_ada).reshape(depth, mod_rows, N_MOD, d)
    tabs = _rope_tables(t_latent)
    for i in range(depth):
        mods = mods_all[i]
        wts = _prep_layer(i, w_in, mla_q_norm, mla_w_uq, mla_kv_norm, mla_w_ukv, gdn_conv_w, gdn_a_log,
                          gdn_dt_bias, gdn_norm, gla_w_gk, gla_b_gk, gla_norm, w_out, ffn_w_in, ffn_b_in,
                          ffn_conv_w, ffn_conv_b, ffn_w_out)
        q, kT, v, gqkv, gz, lq, lk, lv, lg, misc = pre_mix(xs, mods, norm_mix_pre[i][None], wts, tabs)
        mla_l, mla_c = mla_attention(q, kT, v)
        gdn_f, gdn_r, gla_f, gla_r = recurrent_mixers(gqkv, misc, lq, lk, lv, wts)
        x1 = post_mix(xs, mods, mla_l, mla_c, gdn_f, gdn_r, gz, gla_f, gla_r, lg, wts, norm_mix_post[i][None])
        xs = (conv_ffn(x1, mods, norm_ffn_pre[i][None], wts, norm_ffn_post[i][None], keep_context=i < depth - 1),)
    return xs[0]
```
